```python
import math
import jax
import jax.numpy as jnp
from jax import lax
import numpy as np

D_MODEL = 2048
BATCH = 4
SEQ = 4096
DEPTH = 1

HEAD_DIM = 64
N_MIX_HEADS = D_MODEL // HEAD_DIM
SB_HEADS = N_MIX_HEADS // 2
SWA_Q_HEADS = N_MIX_HEADS - SB_HEADS
SWA_KV_HEADS = SWA_Q_HEADS // 4
SWA_GROUP = SWA_Q_HEADS // SWA_KV_HEADS
SB_WIDTH = SB_HEADS * HEAD_DIM
SWA_WIDTH = SWA_Q_HEADS * HEAD_DIM
SWA_KV_WIDTH = SWA_KV_HEADS * HEAD_DIM
MIX_WIDTH = SB_WIDTH + SWA_WIDTH
IN_SPLITS = (SB_WIDTH, SB_WIDTH, SB_WIDTH, SWA_WIDTH, SWA_KV_WIDTH, SWA_KV_WIDTH)
IN_PROJ_WIDTH = 3 * SB_WIDTH + SWA_WIDTH + 2 * SWA_KV_WIDTH
BLOCK = 128
WINDOW = 128
N_META = 16
META_PAD = BLOCK - N_META
NUM_BUCKETS = 32
MAX_DISTANCE = 128
N_GROUPS = 8
EXPERTS_PER_GROUP = 8
N_EXPERTS = N_GROUPS * EXPERTS_PER_GROUP
TOP_K_IN_GROUP = 2
D_EXPERT = D_MODEL // 4
DISPATCH_BLOCK = 128
RMS_EPS = 1e-6

kernel_name = 'hybrid_stickbreak_swa_hier_moe'


def rms_norm(x, gain):
    xf = x.astype(jnp.float32)
    var = jnp.mean(xf * xf, axis=-1, keepdims=True)
    return (xf * lax.rsqrt(var + RMS_EPS)).astype(x.dtype) * gain


def t5_bucket(dist):
    max_exact = NUM_BUCKETS // 2
    d = jnp.maximum(dist, 0)
    large = max_exact + (jnp.log(jnp.maximum(d, 1).astype(jnp.float32) / max_exact)
                         / math.log(MAX_DISTANCE / max_exact)
                         * (NUM_BUCKETS - max_exact)).astype(jnp.int32)
    large = jnp.minimum(large, NUM_BUCKETS - 1)
    return jnp.where(d < max_exact, d, large)


def stick_breaking_attention(q, k, v):
    P = q.shape[2]
    scale = HEAD_DIM ** -0.5
    outs = []
    for i in range(P // BLOCK):
        lo, hi = i * BLOCK, (i + 1) * BLOCK
        qb = q[:, :, lo:hi]
        kb = k[:, :, :hi]
        vb = v[:, :, :hi]
        z = jnp.einsum('bhqd,bhkd->bhqk', qb, kb).astype(jnp.float32) * scale
        q_pos = lo + jnp.arange(BLOCK)[:, None]
        k_pos = jnp.arange(hi)[None, :]
        valid = (k_pos < q_pos) & (k_pos >= META_PAD)
        log_keep = jnp.where(valid, -jax.nn.softplus(z), 0.0)
        rev = lax.cumsum(log_keep, axis=3, reverse=True)
        suffix = jnp.concatenate([rev[..., 1:], jnp.zeros_like(rev[..., :1])], axis=-1)
        log_w = jax.nn.log_sigmoid(z) + suffix
        w = jnp.where(valid, jnp.exp(log_w), 0.0)
        outs.append(jnp.einsum('bhqk,bhkd->bhqd', w.astype(v.dtype), vb))
    return jnp.concatenate(outs, axis=2)


def swa_sink_attention(q, k, v, sinks, rel_bias):
    B, P = q.shape[0], q.shape[1]
    nb = P // BLOCK
    scale = HEAD_DIM ** -0.5
    qb = q.reshape(B, nb, BLOCK, SWA_KV_HEADS, SWA_GROUP, HEAD_DIM)

    def band(t):
        tb = t.reshape(B, nb, BLOCK, SWA_KV_HEADS, HEAD_DIM)
        prev = jnp.concatenate([jnp.zeros_like(tb[:, :1]), tb[:, :-1]], axis=1)
        return jnp.concatenate([prev, tb], axis=2)

    kb, vb = band(k), band(v)
    k_meta = k[:, META_PAD:BLOCK]
    v_meta = v[:, META_PAD:BLOCK]
    q_pos = jnp.arange(nb)[:, None] * BLOCK + jnp.arange(BLOCK)[None, :]
    kb_pos = (jnp.arange(nb)[:, None] - 1) * BLOCK + jnp.arange(2 * BLOCK)[None, :]
    km_pos = jnp.arange(META_PAD, BLOCK)
    d_band = q_pos[:, :, None] - kb_pos[:, None, :]
    ok_band = (d_band >= 0) & (d_band < WINDOW) & (kb_pos[:, None, :] >= BLOCK)
    d_meta = q_pos[:, :, None] - km_pos[None, None, :]
    ok_meta = d_meta >= 0
    dist = jnp.concatenate([d_band, d_meta], axis=-1)
    ok = jnp.concatenate([ok_band, ok_meta], axis=-1)
    bias = rel_bias[t5_bucket(dist)].astype(jnp.float32)
    bias = bias.transpose(3, 0, 1, 2).reshape(SWA_KV_HEADS, SWA_GROUP, nb, BLOCK, -1)
    s_band = jnp.einsum('bnqhgd,bnshd->bhgnqs', qb, kb)
    s_meta = jnp.einsum('bnqhgd,bmhd->bhgnqm', qb, k_meta)
    s = jnp.concatenate([s_band, s_meta], axis=-1).astype(jnp.float32) * scale + bias
    s = jnp.where(ok, s, -jnp.inf)
    sink = sinks.astype(jnp.float32).reshape(SWA_KV_HEADS, SWA_GROUP, 1, 1, 1)
    m = jnp.maximum(jnp.max(s, axis=-1, keepdims=True), sink)
    p = jnp.exp(s - m)
    p = (p / (jnp.sum(p, axis=-1, keepdims=True) + jnp.exp(sink - m))).astype(v.dtype)
    o = (jnp.einsum('bhgnqs,bnshd->bnqhgd', p[..., :2 * BLOCK], vb)
         + jnp.einsum('bhgnqm,bmhd->bnqhgd', p[..., 2 * BLOCK:], v_meta))
    return o.reshape(B, P, SWA_Q_HEADS, HEAD_DIM)


def hierarchical_moe(x, w_rg, b_rg, w_re, b_re, w_gate, w_up, w_down):
    T = x.shape[0]
    g_prob = jax.nn.softmax((x @ w_rg).astype(jnp.float32) + b_rg.astype(jnp.float32), axis=-1)
    g_val, g_idx = lax.top_k(g_prob, 1)
    e_logits = ((x @ w_re).astype(jnp.float32) + b_re.astype(jnp.float32)).reshape(T, N_GROUPS, EXPERTS_PER_GROUP)
    e_logits = e_logits[jnp.arange(T), g_idx[:, 0]]
    e_val, e_idx = lax.top_k(jax.nn.softmax(e_logits, axis=-1), TOP_K_IN_GROUP)
    gate = g_val * e_val / jnp.sum(e_val, axis=-1, keepdims=True)
    expert = g_idx * EXPERTS_PER_GROUP + e_idx
    n_assign = T * TOP_K_IN_GROUP
    a_expert = expert.reshape(-1)
    a_token = jnp.repeat(jnp.arange(T, dtype=jnp.int32), TOP_K_IN_GROUP)
    a_gate = gate.reshape(-1)
    onehot = jax.nn.one_hot(a_expert, N_EXPERTS, dtype=jnp.int32)
    rank = jnp.sum(jnp.cumsum(onehot, axis=0) * onehot, axis=1) - 1
    counts = jnp.sum(onehot, axis=0)
    padded = (counts + DISPATCH_BLOCK - 1) // DISPATCH_BLOCK * DISPATCH_BLOCK
    seg_end = jnp.cumsum(padded)
    seg_start = seg_end - padded
    n_blocks = -(-n_assign // DISPATCH_BLOCK) + N_EXPERTS
    slots = n_blocks * DISPATCH_BLOCK
    dest = seg_start[a_expert] + rank
    slot_token = jnp.full((slots,), T, jnp.int32).at[dest].set(a_token)
    slot_gate = jnp.zeros((slots,), a_gate.dtype).at[dest].set(a_gate)
    block_expert = jnp.minimum(
        jnp.searchsorted(seg_end, jnp.arange(n_blocks) * DISPATCH_BLOCK, side='right'), N_EXPERTS - 1)
    x_pad = jnp.concatenate([x, jnp.zeros_like(x[:1])], axis=0)

    def run_block(args):
        tok, e = args
        xb = x_pad[tok]
        hdn = jax.nn.silu(xb @ w_gate[e]) * (xb @ w_up[e])
        return hdn @ w_down[e]

    y = lax.map(run_block, (slot_token.reshape(n_blocks, DISPATCH_BLOCK), block_expert))
    y = y.reshape(slots, -1) * slot_gate[:, None].astype(y.dtype)
    out = jnp.zeros_like(x_pad).at[slot_token].add(y)
    return out[:T]


def setup_inputs(seed: int = 0) -> dict:
    key = jax.random.key(seed)
    ks = jax.random.split(key, 20)
    f32 = jnp.float32

    def nrm(k, shape, scale):
        return jax.random.normal(k, shape, f32) * scale

    L = DEPTH
    return {
        'x': nrm(ks[0], (BATCH, SEQ, D_MODEL), 1.0),
        'meta_tokens': nrm(ks[1], (N_META, D_MODEL), 1.0),
        'rel_bias': nrm(ks[2], (NUM_BUCKETS, SWA_Q_HEADS), 0.1),
        'ln_mix': 1.0 + nrm(ks[3], (L, D_MODEL), 0.02),
        'w_in': nrm(ks[4], (L, D_MODEL, IN_PROJ_WIDTH), D_MODEL ** -0.5),
        'q_norm': 1.0 + nrm(ks[5], (L, HEAD_DIM), 0.02),
        'k_norm': 1.0 + nrm(ks[6], (L, HEAD_DIM), 0.02),
        'sinks': nrm(ks[7], (L, SWA_Q_HEADS), 0.5),
        'out_norm_sb': 1.0 + nrm(ks[8], (L, SB_WIDTH), 0.02),
        'out_norm_swa': 1.0 + nrm(ks[9], (L, SWA_WIDTH), 0.02),
        'w_out': nrm(ks[10], (L, MIX_WIDTH, D_MODEL), MIX_WIDTH ** -0.5),
        'ln_ffn': 1.0 + nrm(ks[11], (L, D_MODEL), 0.02),
        'w_router_group': nrm(ks[12], (L, D_MODEL, N_GROUPS), D_MODEL ** -0.5),
        'b_router_group': nrm(ks[13], (L, N_GROUPS), 0.01),
        'w_router_expert': nrm(ks[14], (L, D_MODEL, N_EXPERTS), D_MODEL ** -0.5),
        'b_router_expert': nrm(ks[15], (L, N_EXPERTS), 0.01),
        'w_gate': nrm(ks[16], (L, N_EXPERTS, D_MODEL, D_EXPERT), D_MODEL ** -0.5),
        'w_up': nrm(ks[17], (L, N_EXPERTS, D_MODEL, D_EXPERT), D_MODEL ** -0.5),
        'w_down': nrm(ks[18], (L, N_EXPERTS, D_EXPERT, D_MODEL), D_EXPERT ** -0.5),
    }


def reference(x, meta_tokens, rel_bias, ln_mix, w_in, q_norm, k_norm, sinks, out_norm_sb,
              out_norm_swa, w_out, ln_ffn, w_router_group, b_router_group, w_router_expert,
              b_router_expert, w_gate, w_up, w_down):
    B = x.shape[0]
    pad = jnp.zeros((B, META_PAD, D_MODEL), x.dtype)
    meta = jnp.broadcast_to(meta_tokens.astype(x.dtype)[None], (B, N_META, D_MODEL))
    h = jnp.concatenate([pad, meta, x], axis=1)
    P = h.shape[1]
    split_at = [int(i) for i in np.cumsum(IN_SPLITS)[:-1]]
    for l in range(DEPTH):
        hn = rms_norm(h, ln_mix[l])
        proj = hn @ w_in[l]
        q_sb, k_sb, v_sb, q_sw, k_sw, v_sw = jnp.split(proj, split_at, axis=-1)

        def heads_first(t):
            return t.reshape(B, P, SB_HEADS, HEAD_DIM).transpose(0, 2, 1, 3)

        o_sb = stick_breaking_attention(heads_first(q_sb), heads_first(k_sb), heads_first(v_sb))
        o_sb = o_sb.transpose(0, 2, 1, 3).reshape(B, P, SB_WIDTH)
        qs = rms_norm(q_sw.reshape(B, P, SWA_Q_HEADS, HEAD_DIM), q_norm[l])
        ksw = rms_norm(k_sw.reshape(B, P, SWA_KV_HEADS, HEAD_DIM), k_norm[l])
        vsw = v_sw.reshape(B, P, SWA_KV_HEADS, HEAD_DIM)
        o_sw = swa_sink_attention(qs, ksw, vsw, sinks[l], rel_bias).reshape(B, P, SWA_WIDTH)
        mixed = jnp.concatenate([rms_norm(o_sb, out_norm_sb[l]), rms_norm(o_sw, out_norm_swa[l])], axis=-1)
        h = h + mixed @ w_out[l]
        hf = rms_norm(h, ln_ffn[l]).reshape(B * P, D_MODEL)
        y = hierarchical_moe(hf, w_router_group[l], b_router_group[l], w_router_expert[l],
                             b_router_expert[l], w_gate[l], w_up[l], w_down[l])
        h = h + y.reshape(B, P, D_MODEL)
    return h[:, BLOCK:]
```

```python
import functools
import math

import numpy as np
import jax
import jax.numpy as jnp
from jax import lax
from jax.experimental import pallas as pl
from jax.experimental.pallas import tpu as pltpu

F32 = jnp.float32
BF16 = jnp.bfloat16
I32 = jnp.int32
U32 = jnp.uint32

D_MODEL = 2048
HEAD_DIM = 64
LANES = 128
SB_HEADS = 16
SWA_Q_HEADS = 16
SWA_KV_HEADS = 4
SB_WIDTH = SB_HEADS * HEAD_DIM
SWA_WIDTH = SWA_Q_HEADS * HEAD_DIM
SWA_KV_WIDTH = SWA_KV_HEADS * HEAD_DIM
IN_PROJ_WIDTH = 3 * SB_WIDTH + SWA_WIDTH + 2 * SWA_KV_WIDTH
BLOCK = 128
WINDOW = 128
N_META = 16
META_PAD = BLOCK - N_META
NUM_BUCKETS = 32
MAX_DISTANCE = 128
N_GROUPS = 8
EXPERTS_PER_GROUP = 8
N_EXPERTS = N_GROUPS * EXPERTS_PER_GROUP
D_EXPERT = D_MODEL // 4
RMS_EPS = 1e-6
SCALE = HEAD_DIM ** -0.5

COL_Q_SB = 0
COL_K_SB = SB_WIDTH // LANES
COL_V_SB = 2 * SB_WIDTH // LANES
COL_Q_SW = 3 * SB_WIDTH // LANES
COL_K_SW = COL_Q_SW + SWA_WIDTH // LANES
COL_V_SW = COL_K_SW + SWA_KV_WIDTH // LANES

PROJ_TN = 768
SB_CHUNK = 256
MOE_TILE = 256
TOK_TILE = 256
NEG_INF = float("-inf")


def _cparams(sem, vmem_mb):
    return pltpu.CompilerParams(dimension_semantics=sem, vmem_limit_bytes=vmem_mb * 1024 * 1024)


def _norm_proj_body(x_ref, g_ref, w_ref, cs_ref, nf_ref, o_ref, hn_ref, *, n_plain_tiles):
    j = pl.program_id(1)

    @pl.when(j == 0)
    def _():
        x = x_ref[...]
        var = jnp.mean(x * x, axis=-1, keepdims=True)
        hn_ref[...] = (x * lax.rsqrt(var + RMS_EPS) * g_ref[...]).astype(BF16)

    acc = jnp.dot(hn_ref[...], w_ref[...], preferred_element_type=F32)

    @pl.when(j < n_plain_tiles)
    def _():
        o_ref[...] = (acc * cs_ref[...]).astype(o_ref.dtype)

    @pl.when(j >= n_plain_tiles)
    def _():
        lane = lax.broadcasted_iota(I32, (1, LANES), 1)
        lo = lane < HEAD_DIM
        for c in range(acc.shape[1] // LANES):
            sl = slice(c * LANES, (c + 1) * LANES)
            a = acc[:, sl]
            s = a * a
            s_lo = jnp.sum(jnp.where(lo, s, 0.0), axis=-1, keepdims=True)
            s_hi = jnp.sum(jnp.where(lo, 0.0, s), axis=-1, keepdims=True)
            ms = jnp.where(lo, s_lo, s_hi) * (1.0 / HEAD_DIM)
            r = jnp.where(nf_ref[:, sl] > 0.0, lax.rsqrt(ms + RMS_EPS), 1.0)
            o_ref[:, sl] = ((a * r) * cs_ref[:, sl]).astype(o_ref.dtype)


def _norm_proj(x2d, gain, w_bf, col_scale, norm_flag):
    m = x2d.shape[0]
    tm = min(1024, m)
    n = w_bf.shape[1]
    n_tiles = n // PROJ_TN
    n_plain = (3 * SB_WIDTH) // PROJ_TN
    return pl.pallas_call(
        functools.partial(_norm_proj_body, n_plain_tiles=n_plain),
        grid=(m // tm, n_tiles),
        in_specs=[
            pl.BlockSpec((tm, D_MODEL), lambda i, j: (i, 0)),
            pl.BlockSpec((1, D_MODEL), lambda i, j: (0, 0)),
            pl.BlockSpec((D_MODEL, PROJ_TN), lambda i, j: (0, j)),
            pl.BlockSpec((1, PROJ_TN), lambda i, j: (0, j)),
            pl.BlockSpec((1, PROJ_TN), lambda i, j: (0, j)),
        ],
        out_specs=pl.BlockSpec((tm, PROJ_TN), lambda i, j: (i, j)),
        out_shape=jax.ShapeDtypeStruct((m, n), BF16),
        scratch_shapes=[pltpu.VMEM((tm, D_MODEL), BF16)],
        compiler_params=_cparams(("parallel", "arbitrary"), 48),
        name="norm_proj",
    )(x2d, gain, w_bf, col_scale, norm_flag)


def _sb_body(q_ref, k_ref, v_ref, km_ref, vm_ref, o_ref, kx_ref, vx_ref, acc_ref, car_ref, *, seq):
    i = pl.program_id(2)

    @pl.when(i == 0)
    def _():
        kx_ref[0:BLOCK, :] = km_ref[...]
        vx_ref[0:BLOCK, :] = vm_ref[...]
        kx_ref[BLOCK:BLOCK + seq, :] = k_ref[...]
        vx_ref[BLOCK:BLOCK + seq, :] = v_ref[...]
        kx_ref[BLOCK + seq:, :] = jnp.zeros((BLOCK, LANES), BF16)
        vx_ref[BLOCK + seq:, :] = jnp.zeros((BLOCK, LANES), BF16)

    lane = lax.broadcasted_iota(I32, (BLOCK, LANES), 1)
    lo = lane < HEAD_DIM
    q = q_ref[...]
    zq = jnp.zeros_like(q)
    qm = jnp.concatenate([jnp.where(lo, q, zq), jnp.where(lo, zq, q)], axis=0)

    r_i = lax.broadcasted_iota(I32, (2 * BLOCK, SB_CHUNK), 0)
    c_i = lax.broadcasted_iota(I32, (2 * BLOCK, SB_CHUNK), 1)
    tri = (r_i > c_i).astype(F32).astype(BF16)
    q_pos = BLOCK * (i + 1) + (r_i & (BLOCK - 1))

    acc_ref[...] = jnp.zeros_like(acc_ref)
    car_ref[...] = jnp.zeros_like(car_ref)
    c_diag = lax.shift_right_logical(i + 1, 1)

    def step(c, masked):
        start = pl.multiple_of(c * SB_CHUNK, SB_CHUNK)
        kc = kx_ref[pl.ds(start, SB_CHUNK), :]
        vc = vx_ref[pl.ds(start, SB_CHUNK), :]
        s = lax.dot_general(qm, kc, (((1,), (1,)), ((), ())), preferred_element_type=F32)
        l = jnp.log1p(jnp.exp(-jnp.abs(s)))
        log_keep = -(jnp.maximum(s, 0.0) + l)
        log_beta = jnp.minimum(s, 0.0) - l
        if masked:
            k_pos = c * SB_CHUNK + c_i
            valid = (k_pos < q_pos) & (k_pos >= META_PAD)
            log_keep = jnp.where(valid, log_keep, 0.0)
        inner = jnp.dot(log_keep.astype(BF16), tri, preferred_element_type=F32)
        car = car_ref[...]
        log_w = log_beta + inner + jnp.concatenate([car, car], axis=1)
        w = jnp.exp(log_w)
        if masked:
            w = jnp.where(valid, w, 0.0)
        acc_ref[...] += jnp.dot(w.astype(BF16), vc, preferred_element_type=F32)
        car_ref[...] = car + jnp.sum(log_keep, axis=1, keepdims=True)

    step(c_diag, True)

    def mid(t, carry):
        step(c_diag - 1 - t, False)
        return carry

    lax.fori_loop(0, jnp.maximum(c_diag - 1, 0), mid, 0)

    @pl.when(c_diag > 0)
    def _():
        step(0, True)

    acc = acc_ref[...]
    o_ref[...] = jnp.where(lo, acc[0:BLOCK], acc[BLOCK:2 * BLOCK])


def _sb_attention(proj, meta_blk, batch, seq):
    nq = seq // BLOCK
    n_pairs = SB_WIDTH // LANES
    return pl.pallas_call(
        functools.partial(_sb_body, seq=seq),
        grid=(batch, n_pairs, nq),
        in_specs=[
            pl.BlockSpec((BLOCK, LANES), lambda b, h, i: (b * nq + i, COL_Q_SB + h)),
            pl.BlockSpec((seq, LANES), lambda b, h, i: (b, COL_K_SB + h)),
            pl.BlockSpec((seq, LANES), lambda b, h, i: (b, COL_V_SB + h)),
            pl.BlockSpec((BLOCK, LANES), lambda b, h, i: (0, COL_K_SB + h)),
            pl.BlockSpec((BLOCK, LANES), lambda b, h, i: (0, COL_V_SB + h)),
        ],
        out_specs=pl.BlockSpec((BLOCK, LANES), lambda b, h, i: (b * nq + i, h)),
        out_shape=jax.ShapeDtypeStruct((batch * seq, SB_WIDTH), F32),
        scratch_shapes=[
            pltpu.VMEM((seq + 2 * BLOCK, LANES), BF16),
            pltpu.VMEM((seq + 2 * BLOCK, LANES), BF16),
            pltpu.VMEM((2 * BLOCK, LANES), F32),
            pltpu.VMEM((2 * BLOCK, LANES), F32),
        ],
        compiler_params=_cparams(("parallel", "parallel", "arbitrary"), 32),
        name="sb_attn",
    )(proj, proj, proj, meta_blk, meta_blk)


def _swa_body(q_ref, kp_ref, kc_ref, vp_ref, vc_ref, km_ref, vm_ref, bkt_band_ref, bkt_meta_ref,
              rel_ref, sink_ref, o_ref, bias_band_ref, bias_meta_ref):
    t = pl.program_id(1)
    i = pl.program_id(2)
    h0 = 2 * t
    kv_in_hi = (lax.shift_right_logical(t, 1) & 1) == 1

    @pl.when(i == 0)
    def _():
        bb = bkt_band_ref[...]
        bm = bkt_meta_ref[...]
        for hh in range(2):
            band = jnp.full(bb.shape, NEG_INF, F32)
            meta = jnp.full(bm.shape, NEG_INF, F32)
            for b in range(NUM_BUCKETS):
                val = rel_ref[b, h0 + hh]
                band = jnp.where(bb == b, val, band)
                meta = jnp.where(bm == b, val, meta)
            bias_band_ref[hh * BLOCK:(hh + 1) * BLOCK, :] = band
            bias_meta_ref[hh * BLOCK:(hh + 1) * BLOCK, :] = meta

    lane = lax.broadcasted_iota(I32, (BLOCK, LANES), 1)
    lo = lane < HEAD_DIM
    kmask = jnp.logical_xor(lo, kv_in_hi)
    q = q_ref[...]
    q_sw = jnp.concatenate([q[:, HEAD_DIM:], q[:, :HEAD_DIM]], axis=1)
    zq = jnp.zeros_like(q)
    qa = jnp.where(kmask, jnp.where(kv_in_hi, q_sw, q), zq)
    qb = jnp.where(kmask, jnp.where(kv_in_hi, q, q_sw), zq)
    qm = jnp.concatenate([qa, qb], axis=0)

    kb = jnp.concatenate([kp_ref[...], kc_ref[...]], axis=0)
    vb = jnp.concatenate([vp_ref[...], vc_ref[...]], axis=0)
    dn = (((1,), (1,)), ((), ()))
    s_band = lax.dot_general(qm, kb, dn, preferred_element_type=F32) + bias_band_ref[...]
    prev_kill = jnp.where(i == 0, NEG_INF, 0.0)
    col = lax.broadcasted_iota(I32, (2 * BLOCK, 2 * BLOCK), 1)
    s_band = jnp.where(col < BLOCK, s_band + prev_kill, s_band)
    row = lax.broadcasted_iota(I32, (2 * BLOCK, LANES), 0)
    far_a = rel_ref[NUM_BUCKETS - 1, h0]
    far_b = rel_ref[NUM_BUCKETS - 1, h0 + 1]
    far = jnp.where(row < BLOCK, far_a, far_b)
    mcol = lax.broadcasted_iota(I32, (2 * BLOCK, LANES), 1)
    bias_meta = jnp.where(i == 0, bias_meta_ref[...], jnp.where(mcol >= META_PAD, far, NEG_INF))
    s_meta = lax.dot_general(qm, km_ref[...], dn, preferred_element_type=F32) + bias_meta

    rcol = lax.broadcasted_iota(I32, (2 * BLOCK, 1), 0)
    sink = jnp.where(rcol < BLOCK, sink_ref[h0], sink_ref[h0 + 1])
    m = jnp.maximum(jnp.maximum(jnp.max(s_band, axis=-1, keepdims=True),
                                jnp.max(s_meta, axis=-1, keepdims=True)), sink)
    p_band = jnp.exp(s_band - m)
    p_meta = jnp.exp(s_meta - m)
    denom = (jnp.sum(p_band, axis=-1, keepdims=True) + jnp.sum(p_meta, axis=-1, keepdims=True)
             + jnp.exp(sink - m))
    inv = 1.0 / denom
    o2 = (jnp.dot((p_band * inv).astype(BF16), vb, preferred_element_type=F32)
          + jnp.dot((p_meta * inv).astype(BF16), vm_ref[...], preferred_element_type=F32))
    oa = o2[0:BLOCK]
    ob = o2[BLOCK:2 * BLOCK]
    oa_r = pltpu.roll(oa, HEAD_DIM, 1)
    ob_r = pltpu.roll(ob, HEAD_DIM, 1)
    o_ref[...] = jnp.where(lo, jnp.where(kv_in_hi, oa_r, oa), jnp.where(kv_in_hi, ob, ob_r))


def _swa_attention(proj, meta_blk, bkt_band, bkt_meta, rel_bias, sinks, batch, seq):
    nq = seq // BLOCK
    n_pairs = SWA_WIDTH // LANES

    def kv_col(base):
        return lambda b, t, i: (b * nq + i, base + t // 4)

    def kv_col_prev(base):
        return lambda b, t, i: (b * nq + jnp.maximum(i - 1, 0), base + t // 4)

    smem = pl.BlockSpec(memory_space=pltpu.SMEM)
    return pl.pallas_call(
        _swa_body,
        grid=(batch, n_pairs, nq),
        in_specs=[
            pl.BlockSpec((BLOCK, LANES), lambda b, t, i: (b * nq + i, COL_Q_SW + t)),
            pl.BlockSpec((BLOCK, LANES), kv_col_prev(COL_K_SW)),
            pl.BlockSpec((BLOCK, LANES), kv_col(COL_K_SW)),
            pl.BlockSpec((BLOCK, LANES), kv_col_prev(COL_V_SW)),
            pl.BlockSpec((BLOCK, LANES), kv_col(COL_V_SW)),
            pl.BlockSpec((BLOCK, LANES), lambda b, t, i: (0, COL_K_SW + t // 4)),
            pl.BlockSpec((BLOCK, LANES), lambda b, t, i: (0, COL_V_SW + t // 4)),
            pl.BlockSpec((BLOCK, 2 * BLOCK), lambda b, t, i: (0, 0)),
            pl.BlockSpec((BLOCK, LANES), lambda b, t, i: (0, 0)),
            smem,
            smem,
        ],
        out_specs=pl.BlockSpec((BLOCK, LANES), lambda b, t, i: (b * nq + i, t)),
        out_shape=jax.ShapeDtypeStruct((batch * seq, SWA_WIDTH), F32),
        scratch_shapes=[
            pltpu.VMEM((2 * BLOCK, 2 * BLOCK), F32),
            pltpu.VMEM((2 * BLOCK, LANES), F32),
        ],
        compiler_params=_cparams(("parallel", "parallel", "arbitrary"), 32),
        name="swa_attn",
    )(proj, proj, proj, proj, proj, meta_blk, meta_blk, bkt_band, bkt_meta, rel_bias, sinks)


def _t5_bucket_np(dist):
    max_exact = NUM_BUCKETS // 2
    d = np.maximum(dist, 0)
    ratio = np.maximum(d, 1).astype(np.float32) / np.float32(max_exact)
    large = max_exact + (np.log(ratio) / np.float32(math.log(MAX_DISTANCE / max_exact))
                         * np.float32(NUM_BUCKETS - max_exact)).astype(np.int32)
    large = np.minimum(large, NUM_BUCKETS - 1)
    return np.where(d < max_exact, d, large).astype(np.int32)


def _bucket_tables():
    ql = np.arange(BLOCK)[:, None]
    sl = np.arange(2 * BLOCK)[None, :]
    dist = ql + BLOCK - sl
    ok = (dist >= 0) & (dist < WINDOW)
    band = np.where(ok, _t5_bucket_np(dist), -1).astype(np.int32)
    ms = np.arange(BLOCK)[None, :]
    dist_m = ql + BLOCK - ms
    meta = np.where(ms >= META_PAD, _t5_bucket_np(dist_m), -1).astype(np.int32)
    return jnp.asarray(band), jnp.asarray(meta)


def _out_router_body(osb_ref, osw_ref, x_ref, gsb_ref, gsw_ref, wo_ref, lf_ref, wr_ref, br_ref,
                     h1_ref, hfp_ref, route_ref):
    def branch_norm(o_ref_, g_ref_):
        a = o_ref_[...]
        var = jnp.mean(a * a, axis=-1, keepdims=True)
        return ((a * lax.rsqrt(var + RMS_EPS)) * g_ref_[...]).astype(BF16)

    mixed = jnp.concatenate([branch_norm(osb_ref, gsb_ref), branch_norm(osw_ref, gsw_ref)], axis=1)
    h1 = x_ref[...] + jnp.dot(mixed, wo_ref[...], preferred_element_type=F32)
    h1_ref[...] = h1
    var = jnp.mean(h1 * h1, axis=-1, keepdims=True)
    hf = ((h1 * lax.rsqrt(var + RMS_EPS)) * lf_ref[...]).astype(BF16)

    half = D_MODEL // 2
    lo_bits = pltpu.bitcast(hf[:, :half].astype(F32), U32)
    hi_bits = pltpu.bitcast(hf[:, half:].astype(F32), U32)
    hfp_ref[...] = (hi_bits & jnp.uint32(0xFFFF0000)) | (lo_bits >> 16)

    logits = jnp.dot(hf, wr_ref[...], preferred_element_type=F32) + br_ref[...]
    tm = logits.shape[0]
    lane = lax.broadcasted_iota(I32, (tm, LANES), 1)
    big = jnp.int32(LANES)

    def softmax_masked(mask):
        z = jnp.where(mask, logits, NEG_INF)
        zmax = jnp.max(z, axis=-1, keepdims=True)
        e = jnp.exp(z - zmax)
        return e / jnp.sum(e, axis=-1, keepdims=True)

    def top1(p, mask):
        pm = jnp.where(mask, p, -1.0)
        v = jnp.max(pm, axis=-1, keepdims=True)
        idx = jnp.min(jnp.where(pm == v, lane, big), axis=-1, keepdims=True)
        return v, idx

    gmask = lane < N_GROUPS
    g_val, g_idx = top1(softmax_masked(gmask), gmask)
    e_lo = N_GROUPS + EXPERTS_PER_GROUP * g_idx
    emask = (lane >= e_lo) & (lane < e_lo + EXPERTS_PER_GROUP)
    e_prob = softmax_masked(emask)
    v1, i1 = top1(e_prob, emask)
    emask2 = emask & (lane != i1)
    v2, i2 = top1(e_prob, emask2)
    norm = v1 + v2
    gate1 = g_val * v1 / norm
    gate2 = g_val * v2 / norm
    ex1 = (i1 - N_GROUPS).astype(F32)
    ex2 = (i2 - N_GROUPS).astype(F32)
    route_ref[...] = jnp.where(lane == 0, gate1,
                               jnp.where(lane == 1, gate2,
                                         jnp.where(lane == 2, ex1,
                                                   jnp.where(lane == 3, ex2, 0.0))))


def _out_router(o_sb, o_sw, x2d, g_sb, g_sw, wo_bf, ln_ffn, w_router, b_router):
    m = x2d.shape[0]
    tm = TOK_TILE
    row = lambda w: pl.BlockSpec((tm, w), lambda i: (i, 0))
    const = lambda r, c: pl.BlockSpec((r, c), lambda i: (0, 0))
    return pl.pallas_call(
        _out_router_body,
        grid=(m // tm,),
        in_specs=[row(SB_WIDTH), row(SWA_WIDTH), row(D_MODEL), const(1, SB_WIDTH), const(1, SWA_WIDTH),
                  const(D_MODEL, D_MODEL), const(1, D_MODEL), const(D_MODEL, LANES), const(1, LANES)],
        out_specs=[row(D_MODEL), row(D_MODEL // 2), row(LANES)],
        out_shape=[jax.ShapeDtypeStruct((m, D_MODEL), F32),
                   jax.ShapeDtypeStruct((m, D_MODEL // 2), U32),
                   jax.ShapeDtypeStruct((m, LANES), F32)],
        compiler_params=_cparams(("parallel",), 56),
        name="out_router",
    )(o_sb, o_sw, x2d, g_sb, g_sw, wo_bf, ln_ffn, w_router, b_router)


def _dispatch_body(dest_ref, hfp_ref, xs_in_ref, xs_ref, sem):
    del xs_in_ref
    i = pl.program_id(0)
    n = dest_ref.shape[2]

    def row_copy(a):
        tok = i * (n // 2) + lax.shift_right_logical(a, 1)
        return pltpu.make_async_copy(hfp_ref.at[pl.ds(tok, 1), :],
                                     xs_ref.at[pl.ds(dest_ref[0, 0, a], 1), :], sem)

    def start(a, c):
        row_copy(a).start()
        return c

    def wait(a, c):
        row_copy(a).wait()
        return c

    lax.fori_loop(0, n, start, 0)
    lax.fori_loop(0, n, wait, 0)


def _dispatch(dest3, hfp, xs_init):
    n_tiles = dest3.shape[0]
    return pl.pallas_call(
        _dispatch_body,
        grid=(n_tiles,),
        in_specs=[
            pl.BlockSpec((1, 1, dest3.shape[2]), lambda i: (i, 0, 0), memory_space=pltpu.SMEM),
            pl.BlockSpec(memory_space=pl.ANY),
            pl.BlockSpec(memory_space=pl.ANY),
        ],
        out_specs=pl.BlockSpec(memory_space=pl.ANY),
        out_shape=jax.ShapeDtypeStruct(xs_init.shape, xs_init.dtype),
        scratch_shapes=[pltpu.SemaphoreType.DMA],
        input_output_aliases={2: 0},
        compiler_params=_cparams(("arbitrary",), 16),
        name="moe_dispatch",
    )(dest3, hfp, xs_init)


def _experts_body(te_ref, nu_ref, xs_ref, wg_ref, wu_ref, wd_ref, y_ref, wg_bf, wu_bf, wd_bf):
    n = pl.program_id(0)
    n_used = nu_ref[0]

    @pl.when(n < n_used)
    def _():
        prev = te_ref[jnp.maximum(n - 1, 0)]
        fresh = (n == 0) | (te_ref[n] != prev)

        @pl.when(fresh)
        def _():
            wg_bf[...] = wg_ref[0].astype(BF16)
            wu_bf[...] = wu_ref[0].astype(BF16)
            wd_bf[...] = wd_ref[0].astype(BF16)

        u = xs_ref[...]
        x_lo = pltpu.bitcast(u << 16, F32).astype(BF16)
        x_hi = pltpu.bitcast(u & jnp.uint32(0xFFFF0000), F32).astype(BF16)
        x = jnp.concatenate([x_lo, x_hi], axis=1)
        g = jnp.dot(x, wg_bf[...], preferred_element_type=F32)
        up = jnp.dot(x, wu_bf[...], preferred_element_type=F32)
        hdn = (g * jax.nn.sigmoid(g) * up).astype(BF16)
        y_ref[...] = jnp.dot(hdn, wd_bf[...], preferred_element_type=F32)

    @pl.when(n >= n_used)
    def _():
        y_ref[...] = jnp.zeros_like(y_ref)


def _experts(tile_expert, n_used, xs, w_gate, w_up, w_down):
    n_tiles = tile_expert.shape[0]
    slots = xs.shape[0]

    def tile_idx(n, te, nu):
        return jnp.minimum(n, jnp.maximum(nu[0] - 1, 0))

    grid_spec = pltpu.PrefetchScalarGridSpec(
        num_scalar_prefetch=2,
        grid=(n_tiles,),
        in_specs=[
            pl.BlockSpec((MOE_TILE, D_MODEL // 2), lambda n, te, nu: (tile_idx(n, te, nu), 0)),
            pl.BlockSpec((1, D_MODEL, D_EXPERT), lambda n, te, nu: (te[tile_idx(n, te, nu)], 0, 0)),
            pl.BlockSpec((1, D_MODEL, D_EXPERT), lambda n, te, nu: (te[tile_idx(n, te, nu)], 0, 0)),
            pl.BlockSpec((1, D_EXPERT, D_MODEL), lambda n, te, nu: (te[tile_idx(n, te, nu)], 0, 0)),
        ],
        out_specs=pl.BlockSpec((MOE_TILE, D_MODEL), lambda n, te, nu: (n, 0)),
        scratch_shapes=[pltpu.VMEM((D_MODEL, D_EXPERT), BF16),
                        pltpu.VMEM((D_MODEL, D_EXPERT), BF16),
                        pltpu.VMEM((D_EXPERT, D_MODEL), BF16)],
    )
    return pl.pallas_call(
        _experts_body,
        grid_spec=grid_spec,
        out_shape=jax.ShapeDtypeStruct((slots, D_MODEL), F32),
        compiler_params=_cparams(("arbitrary",), 56),
        name="moe_experts",
    )(tile_expert, n_used, xs, w_gate, w_up, w_down)


def _combine_body(dest_ref, h1_ref, route_ref, y_ref, o_ref, ybuf, sem):
    n = dest_ref.shape[2]

    def row_copy(a):
        r = lax.shift_right_logical(a, 1)
        k = a & 1
        return pltpu.make_async_copy(y_ref.at[pl.ds(dest_ref[0, 0, a], 1), :],
                                     ybuf.at[k, pl.ds(r, 1), :], sem)

    def start(a, c):
        row_copy(a).start()
        return c

    def wait(a, c):
        row_copy(a).wait()
        return c

    lax.fori_loop(0, n, start, 0)
    lax.fori_loop(0, n, wait, 0)
    route = route_ref[...]
    g0 = route[:, 0:1]
    g1 = route[:, 1:2]
    o_ref[...] = h1_ref[...] + (ybuf[0] * g0 + ybuf[1] * g1)


def _combine(dest3, h1, route, y):
    m = h1.shape[0]
    tm = dest3.shape[2] // 2
    return pl.pallas_call(
        _combine_body,
        grid=(m // tm,),
        in_specs=[
            pl.BlockSpec((1, 1, 2 * tm), lambda i: (i, 0, 0), memory_space=pltpu.SMEM),
            pl.BlockSpec((tm, D_MODEL), lambda i: (i, 0)),
            pl.BlockSpec((tm, LANES), lambda i: (i, 0)),
            pl.BlockSpec(memory_space=pl.ANY),
        ],
        out_specs=pl.BlockSpec((tm, D_MODEL), lambda i: (i, 0)),
        out_shape=jax.ShapeDtypeStruct((m, D_MODEL), F32),
        scratch_shapes=[pltpu.VMEM((2, tm, D_MODEL), F32), pltpu.SemaphoreType.DMA],
        compiler_params=_cparams(("arbitrary",), 32),
        name="moe_combine",
    )(dest3, h1, route, y)


def _routing_tables(route, n_tok):
    a_expert = route[:, 2:4].astype(I32).reshape(-1)
    onehot = (a_expert[:, None] == jnp.arange(N_EXPERTS, dtype=I32)[None, :]).astype(I32)
    csum = jnp.cumsum(onehot, axis=0)
    rank = jnp.sum(csum * onehot, axis=1) - 1
    counts = csum[-1]
    padded = (counts + MOE_TILE - 1) // MOE_TILE * MOE_TILE
    seg_end = jnp.cumsum(padded)
    seg_start = seg_end - padded
    dest = jnp.sum(onehot * seg_start[None, :], axis=1) + rank
    n_tiles = (2 * n_tok) // MOE_TILE + N_EXPERTS
    tile_expert = jnp.minimum(
        jnp.searchsorted(seg_end, jnp.arange(n_tiles, dtype=I32) * MOE_TILE, side="right"),
        N_EXPERTS - 1).astype(I32)
    n_used = (seg_end[-1] // MOE_TILE).astype(I32).reshape(1)
    return dest.astype(I32), tile_expert, n_used, n_tiles


def kernel(x, meta_tokens, rel_bias, ln_mix, w_in, q_norm, k_norm, sinks, out_norm_sb, out_norm_swa,
           w_out, ln_ffn, w_router_group, b_router_group, w_router_expert, b_router_expert,
           w_gate, w_up, w_down):
    batch, seq, _ = x.shape
    n_tok = batch * seq
    x2d = x.reshape(n_tok, D_MODEL)

    ones = lambda n: jnp.ones((n,), F32)
    col_scale = jnp.concatenate([
        ones(SB_WIDTH) * SCALE, ones(2 * SB_WIDTH),
        jnp.tile(q_norm[0], SWA_Q_HEADS) * SCALE, jnp.tile(k_norm[0], SWA_KV_HEADS), ones(SWA_KV_WIDTH)])[None]
    norm_flag = jnp.concatenate([jnp.zeros((3 * SB_WIDTH,), F32), ones(SWA_WIDTH + SWA_KV_WIDTH),
                                 jnp.zeros((SWA_KV_WIDTH,), F32)])[None]
    w_in_bf = w_in[0].astype(BF16)
    gain_mix = ln_mix[0][None]

    proj = _norm_proj(x2d, gain_mix, w_in_bf, col_scale, norm_flag)
    meta_proj = _norm_proj(meta_tokens.astype(F32), gain_mix, w_in_bf, col_scale, norm_flag)
    meta_blk = jnp.concatenate([jnp.zeros((META_PAD, IN_PROJ_WIDTH), BF16), meta_proj], axis=0)

    o_sb = _sb_attention(proj, meta_blk, batch, seq)
    bkt_band, bkt_meta = _bucket_tables()
    o_sw = _swa_attention(proj, meta_blk, bkt_band, bkt_meta, rel_bias.astype(F32), sinks[0].astype(F32),
                          batch, seq)

    w_router = jnp.concatenate(
        [w_router_group[0], w_router_expert[0],
         jnp.zeros((D_MODEL, LANES - N_GROUPS - N_EXPERTS), F32)], axis=1).astype(BF16)
    b_router = jnp.concatenate(
        [b_router_group[0], b_router_expert[0], jnp.zeros((LANES - N_GROUPS - N_EXPERTS,), F32)])[None]
    h1, hfp, route = _out_router(o_sb, o_sw, x2d, out_norm_sb[0][None], out_norm_swa[0][None],
                                 w_out[0].astype(BF16), ln_ffn[0][None], w_router, b_router)

    dest, tile_expert, n_used, n_tiles = _routing_tables(route, n_tok)
    dest3 = dest.reshape(n_tok // TOK_TILE, 1, 2 * TOK_TILE)
    xs = _dispatch(dest3, hfp, jnp.zeros((n_tiles * MOE_TILE, D_MODEL // 2), U32))
    y = _experts(tile_expert, n_used, xs, w_gate[0], w_up[0], w_down[0])
    out = _combine(dest3, h1, route, y)
    return out.reshape(batch, seq, D_MODEL)
```

```python
import functools
import math

import numpy as np
import jax
import jax.numpy as jnp
from jax import lax
from jax.experimental import pallas as pl
from jax.experimental.pallas import tpu as pltpu

F32 = jnp.float32
BF16 = jnp.bfloat16
I32 = jnp.int32
U32 = jnp.uint32

D_MODEL = 2048
HEAD_DIM = 64
LANES = 128
SB_HEADS = 16
SWA_Q_HEADS = 16
SWA_KV_HEADS = 4
SB_WIDTH = SB_HEADS * HEAD_DIM
SWA_WIDTH = SWA_Q_HEADS * HEAD_DIM
SWA_KV_WIDTH = SWA_KV_HEADS * HEAD_DIM
IN_PROJ_WIDTH = 3 * SB_WIDTH + SWA_WIDTH + 2 * SWA_KV_WIDTH
BLOCK = 128
WINDOW = 128
N_META = 16
META_PAD = BLOCK - N_META
NUM_BUCKETS = 32
MAX_DISTANCE = 128
N_GROUPS = 8
EXPERTS_PER_GROUP = 8
N_EXPERTS = N_GROUPS * EXPERTS_PER_GROUP
D_EXPERT = D_MODEL // 4
RMS_EPS = 1e-6
SCALE = HEAD_DIM ** -0.5

COL_Q_SB = 0
COL_K_SB = SB_WIDTH // LANES
COL_V_SB = 2 * SB_WIDTH // LANES
COL_Q_SW = 3 * SB_WIDTH // LANES
COL_K_SW = COL_Q_SW + SWA_WIDTH // LANES
COL_V_SW = COL_K_SW + SWA_KV_WIDTH // LANES

PROJ_TN = 768
SB_CHUNK = 256
MOE_TILE = 256
TOK_TILE = 256
NEG_INF = float("-inf")


def _cparams(sem, vmem_mb):
    return pltpu.CompilerParams(dimension_semantics=sem, vmem_limit_bytes=vmem_mb * 1024 * 1024)


def _norm_proj_body(x_ref, g_ref, w_ref, cs_ref, nf_ref, o_ref, hn_ref, *, n_plain_tiles):
    j = pl.program_id(1)

    @pl.when(j == 0)
    def _():
        x = x_ref[...]
        var = jnp.mean(x * x, axis=-1, keepdims=True)
        hn_ref[...] = (x * lax.rsqrt(var + RMS_EPS) * g_ref[...]).astype(BF16)

    acc = jnp.dot(hn_ref[...], w_ref[...], preferred_element_type=F32)

    @pl.when(j < n_plain_tiles)
    def _():
        o_ref[...] = (acc * cs_ref[...]).astype(o_ref.dtype)

    @pl.when(j >= n_plain_tiles)
    def _():
        lane = lax.broadcasted_iota(I32, (1, LANES), 1)
        lo = lane < HEAD_DIM
        for c in range(acc.shape[1] // LANES):
            sl = slice(c * LANES, (c + 1) * LANES)
            a = acc[:, sl]
            s = a * a
            s_lo = jnp.sum(jnp.where(lo, s, 0.0), axis=-1, keepdims=True)
            s_hi = jnp.sum(jnp.where(lo, 0.0, s), axis=-1, keepdims=True)
            ms = jnp.where(lo, s_lo, s_hi) * (1.0 / HEAD_DIM)
            r = jnp.where(nf_ref[:, sl] > 0.0, lax.rsqrt(ms + RMS_EPS), 1.0)
            o_ref[:, sl] = ((a * r) * cs_ref[:, sl]).astype(o_ref.dtype)


def _norm_proj(x2d, gain, w_bf, col_scale, norm_flag):
    m = x2d.shape[0]
    tm = min(1024, m)
    n = w_bf.shape[1]
    n_tiles = n // PROJ_TN
    n_plain = (3 * SB_WIDTH) // PROJ_TN
    return pl.pallas_call(
        functools.partial(_norm_proj_body, n_plain_tiles=n_plain),
        grid=(m // tm, n_tiles),
        in_specs=[
            pl.BlockSpec((tm, D_MODEL), lambda i, j: (i, 0)),
            pl.BlockSpec((1, D_MODEL), lambda i, j: (0, 0)),
            pl.BlockSpec((D_MODEL, PROJ_TN), lambda i, j: (0, j)),
            pl.BlockSpec((1, PROJ_TN), lambda i, j: (0, j)),
            pl.BlockSpec((1, PROJ_TN), lambda i, j: (0, j)),
        ],
        out_specs=pl.BlockSpec((tm, PROJ_TN), lambda i, j: (i, j)),
        out_shape=jax.ShapeDtypeStruct((m, n), BF16),
        scratch_shapes=[pltpu.VMEM((tm, D_MODEL), BF16)],
        compiler_params=_cparams(("parallel", "arbitrary"), 48),
        name="norm_proj",
    )(x2d, gain, w_bf, col_scale, norm_flag)


SB_SKIP_LOG = 104.0


def _sb_body(q_ref, k_ref, v_ref, km_ref, vm_ref, o_ref, kx_ref, vx_ref, acc_ref, car_ref, *, seq):
    i = pl.program_id(2)

    @pl.when(i == 0)
    def _():
        zeros = jnp.zeros((BLOCK, LANES), BF16)
        kx_ref[0:BLOCK, :] = zeros
        vx_ref[0:BLOCK, :] = zeros
        kx_ref[BLOCK:2 * BLOCK, :] = km_ref[...]
        vx_ref[BLOCK:2 * BLOCK, :] = vm_ref[...]
        kx_ref[2 * BLOCK:, :] = k_ref[...]
        vx_ref[2 * BLOCK:, :] = v_ref[...]

    lane = lax.broadcasted_iota(I32, (BLOCK, LANES), 1)
    lo = lane < HEAD_DIM
    q = q_ref[...]
    zq = jnp.zeros_like(q)
    qm = jnp.concatenate([jnp.where(lo, q, zq), jnp.where(lo, zq, q)], axis=0)

    def later_key_matrix(nk):
        r = lax.broadcasted_iota(I32, (nk, nk), 0)
        c = lax.broadcasted_iota(I32, (nk, nk), 1)
        return (r > c).astype(F32).astype(BF16)

    tri = {BLOCK: later_key_matrix(BLOCK), SB_CHUNK: later_key_matrix(SB_CHUNK)}

    acc_ref[...] = jnp.zeros_like(acc_ref)
    car_ref[...] = jnp.zeros_like(car_ref)
    own = pl.multiple_of(BLOCK * (i + 2), BLOCK)

    def step(start, nk, mask):
        kc = kx_ref[pl.ds(start, nk), :]
        vc = vx_ref[pl.ds(start, nk), :]
        s = lax.dot_general(qm, kc, (((1,), (1,)), ((), ())), preferred_element_type=F32)
        sp = jnp.maximum(s, 0.0) + jnp.log(1.0 + jnp.exp(-jnp.abs(s)))
        log_beta = s - sp
        if mask is not None:
            r_i = lax.broadcasted_iota(I32, (2 * BLOCK, nk), 0)
            c_i = lax.broadcasted_iota(I32, (2 * BLOCK, nk), 1)
            if mask == "diag":
                valid = c_i < (r_i & (BLOCK - 1))
            else:
                valid = (start + c_i) >= (BLOCK + META_PAD)
            sp = jnp.where(valid, sp, 0.0)
        inner = jnp.dot(sp.astype(BF16), tri[nk], preferred_element_type=F32)
        car = car_ref[...]
        car_k = car if nk == LANES else jnp.concatenate([car, car], axis=1)
        w = jnp.exp(log_beta - inner - car_k)
        if mask is not None:
            w = jnp.where(valid, w, 0.0)
        acc_ref[...] += jnp.dot(w.astype(BF16), vc, preferred_element_type=F32)
        new_car = car + jnp.sum(sp, axis=1, keepdims=True)
        car_ref[...] = new_car
        return jnp.min(jnp.min(new_car, axis=0, keepdims=True))

    least = step(own, BLOCK, "diag")
    n_low = lax.shift_right_logical(i + 2, 1)

    def cond(st):
        j, least_ = st
        return (j < n_low) & (least_ <= SB_SKIP_LOG)

    def body(st):
        j, _ = st
        start = pl.multiple_of(own - j * SB_CHUNK, BLOCK)
        return j + 1, step(start, SB_CHUNK, None)

    j_end, least = lax.while_loop(cond, body, (jnp.int32(1), least))

    @pl.when((j_end == n_low) & (least <= SB_SKIP_LOG))
    def _():
        step(pl.multiple_of(own - n_low * SB_CHUNK, BLOCK), SB_CHUNK, "low")

    acc = acc_ref[...]
    o_ref[...] = jnp.where(lo, acc[0:BLOCK], acc[BLOCK:2 * BLOCK])


def _sb_attention(proj, meta_blk, batch, seq):
    nq = seq // BLOCK
    n_pairs = SB_WIDTH // LANES
    return pl.pallas_call(
        functools.partial(_sb_body, seq=seq),
        grid=(batch, n_pairs, nq),
        in_specs=[
            pl.BlockSpec((BLOCK, LANES), lambda b, h, i: (b * nq + i, COL_Q_SB + h)),
            pl.BlockSpec((seq, LANES), lambda b, h, i: (b, COL_K_SB + h)),
            pl.BlockSpec((seq, LANES), lambda b, h, i: (b, COL_V_SB + h)),
            pl.BlockSpec((BLOCK, LANES), lambda b, h, i: (0, COL_K_SB + h)),
            pl.BlockSpec((BLOCK, LANES), lambda b, h, i: (0, COL_V_SB + h)),
        ],
        out_specs=pl.BlockSpec((BLOCK, LANES), lambda b, h, i: (b * nq + i, h)),
        out_shape=jax.ShapeDtypeStruct((batch * seq, SB_WIDTH), F32),
        scratch_shapes=[
            pltpu.VMEM((seq + 2 * BLOCK, LANES), BF16),
            pltpu.VMEM((seq + 2 * BLOCK, LANES), BF16),
            pltpu.VMEM((2 * BLOCK, LANES), F32),
            pltpu.VMEM((2 * BLOCK, LANES), F32),
        ],
        compiler_params=_cparams(("parallel", "parallel", "arbitrary"), 32),
        name="sb_attn",
    )(proj, proj, proj, meta_blk, meta_blk)


def _swa_body(q_ref, kp_ref, kc_ref, vp_ref, vc_ref, km_ref, vm_ref, bkt_ref,
              rel_ref, sink_ref, o_ref, bias_ref):
    t = pl.program_id(1)
    i = pl.program_id(2)
    h0 = 2 * t
    kv_in_hi = (lax.shift_right_logical(t, 1) & 1) == 1

    @pl.when(i <= 1)
    def _():
        bkt = bkt_ref[jnp.minimum(i, 1)]
        for hh in range(2):
            bias = jnp.full(bkt.shape, NEG_INF, F32)
            for b in range(NUM_BUCKETS):
                bias = jnp.where(bkt == b, rel_ref[b, h0 + hh], bias)
            bias_ref[hh * BLOCK:(hh + 1) * BLOCK, :] = bias

    lane = lax.broadcasted_iota(I32, (BLOCK, LANES), 1)
    lo = lane < HEAD_DIM
    kmask = jnp.logical_xor(lo, kv_in_hi)
    q = q_ref[...]
    q_sw = jnp.concatenate([q[:, HEAD_DIM:], q[:, :HEAD_DIM]], axis=1)
    zq = jnp.zeros_like(q)
    qa = jnp.where(kmask, jnp.where(kv_in_hi, q_sw, q), zq)
    qb = jnp.where(kmask, jnp.where(kv_in_hi, q, q_sw), zq)
    qm = jnp.concatenate([qa, qb], axis=0)

    k_all = jnp.concatenate([kp_ref[...], kc_ref[...], km_ref[...]], axis=0)
    v_all = jnp.concatenate([vp_ref[...], vc_ref[...], vm_ref[...]], axis=0)
    s = lax.dot_general(qm, k_all, (((1,), (1,)), ((), ())), preferred_element_type=F32) + bias_ref[...]

    rcol = lax.broadcasted_iota(I32, (2 * BLOCK, 1), 0)
    sink = jnp.where(rcol < BLOCK, sink_ref[h0], sink_ref[h0 + 1])
    m = jnp.maximum(jnp.max(s, axis=-1, keepdims=True), sink)
    p = jnp.exp(s - m)
    denom = jnp.sum(p, axis=-1, keepdims=True) + jnp.exp(sink - m)
    o2 = jnp.dot((p * (1.0 / denom)).astype(BF16), v_all, preferred_element_type=F32)
    oa = o2[0:BLOCK]
    ob = o2[BLOCK:2 * BLOCK]
    oa_r = pltpu.roll(oa, HEAD_DIM, 1)
    ob_r = pltpu.roll(ob, HEAD_DIM, 1)
    o_ref[...] = jnp.where(lo, jnp.where(kv_in_hi, oa_r, oa), jnp.where(kv_in_hi, ob, ob_r))


def _swa_attention(proj, meta_blk, bkt, rel_bias, sinks, batch, seq):
    nq = seq // BLOCK
    n_pairs = SWA_WIDTH // LANES

    def kv_col(base):
        return lambda b, t, i: (b * nq + i, base + t // 4)

    def kv_col_prev(base):
        return lambda b, t, i: (b * nq + jnp.maximum(i - 1, 0), base + t // 4)

    smem = pl.BlockSpec(memory_space=pltpu.SMEM)
    return pl.pallas_call(
        _swa_body,
        grid=(batch, n_pairs, nq),
        in_specs=[
            pl.BlockSpec((BLOCK, LANES), lambda b, t, i: (b * nq + i, COL_Q_SW + t)),
            pl.BlockSpec((BLOCK, LANES), kv_col_prev(COL_K_SW)),
            pl.BlockSpec((BLOCK, LANES), kv_col(COL_K_SW)),
            pl.BlockSpec((BLOCK, LANES), kv_col_prev(COL_V_SW)),
            pl.BlockSpec((BLOCK, LANES), kv_col(COL_V_SW)),
            pl.BlockSpec((BLOCK, LANES), lambda b, t, i: (0, COL_K_SW + t // 4)),
            pl.BlockSpec((BLOCK, LANES), lambda b, t, i: (0, COL_V_SW + t // 4)),
            pl.BlockSpec((2, BLOCK, 3 * BLOCK), lambda b, t, i: (0, 0, 0)),
            smem,
            smem,
        ],
        out_specs=pl.BlockSpec((BLOCK, LANES), lambda b, t, i: (b * nq + i, t)),
        out_shape=jax.ShapeDtypeStruct((batch * seq, SWA_WIDTH), F32),
        scratch_shapes=[pltpu.VMEM((2 * BLOCK, 3 * BLOCK), F32)],
        compiler_params=_cparams(("parallel", "parallel", "arbitrary"), 32),
        name="swa_attn",
    )(proj, proj, proj, proj, proj, meta_blk, meta_blk, bkt, rel_bias, sinks)


def _t5_bucket_np(dist):
    max_exact = NUM_BUCKETS // 2
    d = np.maximum(dist, 0)
    ratio = np.maximum(d, 1).astype(np.float32) / np.float32(max_exact)
    large = max_exact + (np.log(ratio) / np.float32(math.log(MAX_DISTANCE / max_exact))
                         * np.float32(NUM_BUCKETS - max_exact)).astype(np.int32)
    large = np.minimum(large, NUM_BUCKETS - 1)
    return np.where(d < max_exact, d, large).astype(np.int32)


def _bucket_tables():
    ql = np.arange(BLOCK)[:, None]
    sl = np.arange(2 * BLOCK)[None, :]
    dist = ql + BLOCK - sl
    ok = (dist >= 0) & (dist < WINDOW)
    band = np.where(ok, _t5_bucket_np(dist), -1).astype(np.int32)
    band_first = np.where(sl >= BLOCK, band, -1).astype(np.int32)
    ms = np.arange(BLOCK)[None, :]
    tables = []
    for blk, b in ((0, band_first), (1, band)):
        dist_m = ql + (blk + 1) * BLOCK - ms
        meta = np.where(ms >= META_PAD, _t5_bucket_np(dist_m), -1).astype(np.int32)
        tables.append(np.concatenate([b, meta], axis=1))
    return jnp.asarray(np.stack(tables))


def _out_router_body(osb_ref, osw_ref, x_ref, gsb_ref, gsw_ref, wo_ref, lf_ref, wr_ref, br_ref,
                     h1_ref, hfp_ref, route_ref):
    def branch_norm(o_ref_, g_ref_):
        a = o_ref_[...]
        var = jnp.mean(a * a, axis=-1, keepdims=True)
        return ((a * lax.rsqrt(var + RMS_EPS)) * g_ref_[...]).astype(BF16)

    mixed = jnp.concatenate([branch_norm(osb_ref, gsb_ref), branch_norm(osw_ref, gsw_ref)], axis=1)
    h1 = x_ref[...] + jnp.dot(mixed, wo_ref[...], preferred_element_type=F32)
    h1_ref[...] = h1
    var = jnp.mean(h1 * h1, axis=-1, keepdims=True)
    hf = ((h1 * lax.rsqrt(var + RMS_EPS)) * lf_ref[...]).astype(BF16)

    half = D_MODEL // 2
    lo_bits = pltpu.bitcast(hf[:, :half].astype(F32), U32)
    hi_bits = pltpu.bitcast(hf[:, half:].astype(F32), U32)
    hfp_ref[...] = (hi_bits & jnp.uint32(0xFFFF0000)) | (lo_bits >> 16)

    logits = jnp.dot(hf, wr_ref[...], preferred_element_type=F32) + br_ref[...]
    tm = logits.shape[0]
    lane = lax.broadcasted_iota(I32, (tm, LANES), 1)
    big = jnp.int32(LANES)

    def softmax_masked(mask):
        z = jnp.where(mask, logits, NEG_INF)
        zmax = jnp.max(z, axis=-1, keepdims=True)
        e = jnp.exp(z - zmax)
        return e / jnp.sum(e, axis=-1, keepdims=True)

    def top1(p, mask):
        pm = jnp.where(mask, p, -1.0)
        v = jnp.max(pm, axis=-1, keepdims=True)
        idx = jnp.min(jnp.where(pm == v, lane, big), axis=-1, keepdims=True)
        return v, idx

    gmask = lane < N_GROUPS
    g_val, g_idx = top1(softmax_masked(gmask), gmask)
    e_lo = N_GROUPS + EXPERTS_PER_GROUP * g_idx
    emask = (lane >= e_lo) & (lane < e_lo + EXPERTS_PER_GROUP)
    e_prob = softmax_masked(emask)
    v1, i1 = top1(e_prob, emask)
    emask2 = emask & (lane != i1)
    v2, i2 = top1(e_prob, emask2)
    norm = v1 + v2
    gate1 = g_val * v1 / norm
    gate2 = g_val * v2 / norm
    ex1 = (i1 - N_GROUPS).astype(F32)
    ex2 = (i2 - N_GROUPS).astype(F32)
    route_ref[...] = jnp.where(lane == 0, gate1,
                               jnp.where(lane == 1, gate2,
                                         jnp.where(lane == 2, ex1,
                                                   jnp.where(lane == 3, ex2, 0.0))))


def _out_router(o_sb, o_sw, x2d, g_sb, g_sw, wo_bf, ln_ffn, w_router, b_router):
    m = x2d.shape[0]
    tm = TOK_TILE
    row = lambda w: pl.BlockSpec((tm, w), lambda i: (i, 0))
    const = lambda r, c: pl.BlockSpec((r, c), lambda i: (0, 0))
    return pl.pallas_call(
        _out_router_body,
        grid=(m // tm,),
        in_specs=[row(SB_WIDTH), row(SWA_WIDTH), row(D_MODEL), const(1, SB_WIDTH), const(1, SWA_WIDTH),
                  const(D_MODEL, D_MODEL), const(1, D_MODEL), const(D_MODEL, LANES), const(1, LANES)],
        out_specs=[row(D_MODEL), row(D_MODEL // 2), row(LANES)],
        out_shape=[jax.ShapeDtypeStruct((m, D_MODEL), F32),
                   jax.ShapeDtypeStruct((m, D_MODEL // 2), U32),
                   jax.ShapeDtypeStruct((m, LANES), F32)],
        compiler_params=_cparams(("parallel",), 56),
        name="out_router",
    )(o_sb, o_sw, x2d, g_sb, g_sw, wo_bf, ln_ffn, w_router, b_router)


def _experts_body(te_ref, nu_ref, src_ref, srcn_ref, hfp_ref, wg_ref, wu_ref, wd_ref, y_ref,
                  xbuf, wg_bf, wu_bf, wd_bf, gsem):
    n = pl.program_id(0)
    n_used = nu_ref[0]

    def row_copy(idx_ref, r, slot):
        return pltpu.make_async_copy(hfp_ref.at[pl.ds(idx_ref[0, 0, r], 1), :],
                                     xbuf.at[slot, pl.ds(r, 1), :], gsem.at[slot])

    @pl.when(n == 0)
    def _():
        for r in range(MOE_TILE):
            row_copy(src_ref, r, 0).start()

    @pl.when(n < n_used)
    def _():
        prev = te_ref[jnp.maximum(n - 1, 0)]
        fresh = (n == 0) | (te_ref[n] != prev)

        @pl.when(fresh)
        def _():
            wg_bf[...] = wg_ref[0].astype(BF16)
            wu_bf[...] = wu_ref[0].astype(BF16)
            wd_bf[...] = wd_ref[0].astype(BF16)

        slot = n & 1
        for r in range(MOE_TILE):
            row_copy(srcn_ref, r, 1 - slot).start()
        for r in range(MOE_TILE):
            row_copy(src_ref, r, slot).wait()

        u = xbuf[slot]
        x_lo = pltpu.bitcast(u << 16, F32).astype(BF16)
        x_hi = pltpu.bitcast(u & jnp.uint32(0xFFFF0000), F32).astype(BF16)
        x = jnp.concatenate([x_lo, x_hi], axis=1)
        g = jnp.dot(x, wg_bf[...], preferred_element_type=F32)
        up = jnp.dot(x, wu_bf[...], preferred_element_type=F32)
        hdn = (g * jax.nn.sigmoid(g) * up).astype(BF16)
        y_ref[...] = jnp.dot(hdn, wd_bf[...], preferred_element_type=F32)

    @pl.when(n == n_used)
    def _():
        for r in range(MOE_TILE):
            row_copy(src_ref, r, n & 1).wait()

    @pl.when(n >= n_used)
    def _():
        y_ref[...] = jnp.zeros_like(y_ref)


def _experts(tile_expert, n_used, src3, hfp, w_gate, w_up, w_down):
    n_tiles = tile_expert.shape[0]

    def tile_idx(n, te, nu):
        return jnp.minimum(n, jnp.maximum(nu[0] - 1, 0))

    def w_spec(shape):
        return pl.BlockSpec((1,) + shape, lambda n, te, nu: (te[tile_idx(n, te, nu)], 0, 0))

    grid_spec = pltpu.PrefetchScalarGridSpec(
        num_scalar_prefetch=2,
        grid=(n_tiles + 1,),
        in_specs=[
            pl.BlockSpec((1, 1, MOE_TILE), lambda n, te, nu: (n, 0, 0), memory_space=pltpu.SMEM),
            pl.BlockSpec((1, 1, MOE_TILE), lambda n, te, nu: (n + 1, 0, 0), memory_space=pltpu.SMEM),
            pl.BlockSpec(memory_space=pl.ANY),
            w_spec((D_MODEL, D_EXPERT)),
            w_spec((D_MODEL, D_EXPERT)),
            w_spec((D_EXPERT, D_MODEL)),
        ],
        out_specs=pl.BlockSpec((MOE_TILE, D_MODEL), lambda n, te, nu: (n, 0)),
        scratch_shapes=[pltpu.VMEM((2, MOE_TILE, D_MODEL // 2), U32),
                        pltpu.VMEM((D_MODEL, D_EXPERT), BF16),
                        pltpu.VMEM((D_MODEL, D_EXPERT), BF16),
                        pltpu.VMEM((D_EXPERT, D_MODEL), BF16),
                        pltpu.SemaphoreType.DMA((2,))],
    )
    return pl.pallas_call(
        _experts_body,
        grid_spec=grid_spec,
        out_shape=jax.ShapeDtypeStruct(((n_tiles + 1) * MOE_TILE, D_MODEL), F32),
        compiler_params=_cparams(("arbitrary",), 56),
        name="moe_experts",
    )(tile_expert, n_used, src3, src3, hfp, w_gate, w_up, w_down)


def _combine_body(dcur_ref, dnxt_ref, h1_ref, route_ref, y_ref, o_ref, ybuf, sem, *, n_steps):
    i = pl.program_id(0)
    n = dcur_ref.shape[2]

    def row_copy(idx_ref, a, slot):
        return pltpu.make_async_copy(y_ref.at[pl.ds(idx_ref[0, 0, a], 1), :],
                                     ybuf.at[slot, a % 2, pl.ds(a // 2, 1), :], sem.at[slot])

    @pl.when(i == 0)
    def _():
        for a in range(n):
            row_copy(dcur_ref, a, 0).start()

    @pl.when(i < n_steps)
    def _():
        slot = i & 1
        for a in range(n):
            row_copy(dnxt_ref, a, 1 - slot).start()
        for a in range(n):
            row_copy(dcur_ref, a, slot).wait()
        route = route_ref[...]
        g0 = route[:, 0:1]
        g1 = route[:, 1:2]
        o_ref[...] = h1_ref[...] + (ybuf[slot, 0] * g0 + ybuf[slot, 1] * g1)

    @pl.when(i == n_steps)
    def _():
        for a in range(n):
            row_copy(dcur_ref, a, i & 1).wait()


def _combine(dest3, h1, route, y):
    m = h1.shape[0]
    tm = dest3.shape[2] // 2
    n_steps = m // tm
    tok = lambda i: (jnp.minimum(i, n_steps - 1), 0)
    return pl.pallas_call(
        functools.partial(_combine_body, n_steps=n_steps),
        grid=(n_steps + 1,),
        in_specs=[
            pl.BlockSpec((1, 1, 2 * tm), lambda i: (i, 0, 0), memory_space=pltpu.SMEM),
            pl.BlockSpec((1, 1, 2 * tm), lambda i: (jnp.minimum(i + 1, n_steps), 0, 0), memory_space=pltpu.SMEM),
            pl.BlockSpec((tm, D_MODEL), tok),
            pl.BlockSpec((tm, LANES), tok),
            pl.BlockSpec(memory_space=pl.ANY),
        ],
        out_specs=pl.BlockSpec((tm, D_MODEL), tok),
        out_shape=jax.ShapeDtypeStruct((m, D_MODEL), F32),
        scratch_shapes=[pltpu.VMEM((2, 2, tm, D_MODEL), F32), pltpu.SemaphoreType.DMA((2,))],
        compiler_params=_cparams(("arbitrary",), 40),
        name="moe_combine",
    )(dest3, dest3, h1, route, y)


def _routing_tables(route, n_tok):
    a_expert = route[:, 2:4].astype(I32).reshape(-1)
    n_assign = a_expert.shape[0]
    onehot = (a_expert[:, None] == jnp.arange(N_EXPERTS, dtype=I32)[None, :]).astype(I32)
    csum = jnp.cumsum(onehot, axis=0)
    rank = jnp.sum(csum * onehot, axis=1) - 1
    counts = csum[-1]
    padded = (counts + MOE_TILE - 1) // MOE_TILE * MOE_TILE
    seg_end = jnp.cumsum(padded)
    seg_start = seg_end - padded
    first = jnp.cumsum(counts) - counts
    dest = jnp.sum(onehot * seg_start[None, :], axis=1) + rank
    n_tiles = n_assign // MOE_TILE + N_EXPERTS
    tile_expert = jnp.minimum(
        jnp.searchsorted(seg_end, jnp.arange(n_tiles, dtype=I32) * MOE_TILE, side="right"),
        N_EXPERTS - 1).astype(I32)
    n_used = (seg_end[-1] // MOE_TILE).astype(I32)
    order = jnp.argsort(a_expert, stable=True).astype(I32)
    slot = jnp.arange((n_tiles + 2) * MOE_TILE, dtype=I32)
    tile = slot // MOE_TILE
    e = tile_expert[jnp.minimum(tile, n_tiles - 1)]
    j = slot - seg_start[e]
    valid = (tile < n_used) & (j < counts[e])
    a = order[jnp.clip(first[e] + j, 0, n_assign - 1)]
    src = jnp.where(valid, a // 2, 0).astype(I32)
    return dest.astype(I32), src, tile_expert, n_used.reshape(1)


def kernel(x, meta_tokens, rel_bias, ln_mix, w_in, q_norm, k_norm, sinks, out_norm_sb, out_norm_swa,
           w_out, ln_ffn, w_router_group, b_router_group, w_router_expert, b_router_expert,
           w_gate, w_up, w_down):
    batch, seq, _ = x.shape
    n_tok = batch * seq
    x2d = x.reshape(n_tok, D_MODEL)

    ones = lambda n: jnp.ones((n,), F32)
    col_scale = jnp.concatenate([
        ones(SB_WIDTH) * SCALE, ones(2 * SB_WIDTH),
        jnp.tile(q_norm[0], SWA_Q_HEADS) * SCALE, jnp.tile(k_norm[0], SWA_KV_HEADS), ones(SWA_KV_WIDTH)])[None]
    norm_flag = jnp.concatenate([jnp.zeros((3 * SB_WIDTH,), F32), ones(SWA_WIDTH + SWA_KV_WIDTH),
                                 jnp.zeros((SWA_KV_WIDTH,), F32)])[None]
    w_in_bf = w_in[0].astype(BF16)
    gain_mix = ln_mix[0][None]

    proj = _norm_proj(x2d, gain_mix, w_in_bf, col_scale, norm_flag)
    meta_proj = _norm_proj(meta_tokens.astype(F32), gain_mix, w_in_bf, col_scale, norm_flag)
    meta_blk = jnp.concatenate([jnp.zeros((META_PAD, IN_PROJ_WIDTH), BF16), meta_proj], axis=0)

    o_sb = _sb_attention(proj, meta_blk, batch, seq)
    o_sw = _swa_attention(proj, meta_blk, _bucket_tables(), rel_bias.astype(F32), sinks[0].astype(F32),
                          batch, seq)

    w_router = jnp.concatenate(
        [w_router_group[0], w_router_expert[0],
         jnp.zeros((D_MODEL, LANES - N_GROUPS - N_EXPERTS), F32)], axis=1).astype(BF16)
    b_router = jnp.concatenate(
        [b_router_group[0], b_router_expert[0], jnp.zeros((LANES - N_GROUPS - N_EXPERTS,), F32)])[None]
    h1, hfp, route = _out_router(o_sb, o_sw, x2d, out_norm_sb[0][None], out_norm_swa[0][None],
                                 w_out[0].astype(BF16), ln_ffn[0][None], w_router, b_router)

    dest, src, tile_expert, n_used = _routing_tables(route, n_tok)
    y = _experts(tile_expert, n_used, src.reshape(-1, 1, MOE_TILE), hfp, w_gate[0], w_up[0], w_down[0])
    dest3 = jnp.concatenate([dest, jnp.zeros((2 * TOK_TILE,), I32)]).reshape(-1, 1, 2 * TOK_TILE)
    out = _combine(dest3, h1, route, y)
    return out.reshape(batch, seq, D_MODEL)
```

```python
import functools
import math

import numpy as np
import jax
import jax.numpy as jnp
from jax import lax
from jax.experimental import pallas as pl
from jax.experimental.pallas import tpu as pltpu

F32 = jnp.float32
BF16 = jnp.bfloat16
I32 = jnp.int32
U32 = jnp.uint32

D_MODEL = 2048
HEAD_DIM = 64
LANES = 128
SB_HEADS = 16
SWA_Q_HEADS = 16
SWA_KV_HEADS = 4
SB_WIDTH = SB_HEADS * HEAD_DIM
SWA_WIDTH = SWA_Q_HEADS * HEAD_DIM
SWA_KV_WIDTH = SWA_KV_HEADS * HEAD_DIM
IN_PROJ_WIDTH = 3 * SB_WIDTH + SWA_WIDTH + 2 * SWA_KV_WIDTH
BLOCK = 128
WINDOW = 128
N_META = 16
META_PAD = BLOCK - N_META
NUM_BUCKETS = 32
MAX_DISTANCE = 128
N_GROUPS = 8
EXPERTS_PER_GROUP = 8
N_EXPERTS = N_GROUPS * EXPERTS_PER_GROUP
D_EXPERT = D_MODEL // 4
RMS_EPS = 1e-6
SCALE = HEAD_DIM ** -0.5

COL_Q_SB = 0
COL_K_SB = SB_WIDTH // LANES
COL_V_SB = 2 * SB_WIDTH // LANES
COL_Q_SW = 3 * SB_WIDTH // LANES
COL_K_SW = COL_Q_SW + SWA_WIDTH // LANES
COL_V_SW = COL_K_SW + SWA_KV_WIDTH // LANES

PROJ_TN = 768
SB_CHUNK = 256
MOE_TILE = 256
TOK_TILE = 256
NEG_INF = float("-inf")


def _cparams(sem, vmem_mb):
    return pltpu.CompilerParams(dimension_semantics=sem, vmem_limit_bytes=vmem_mb * 1024 * 1024)


def _norm_proj_body(x_ref, g_ref, w_ref, cs_ref, nf_ref, o_ref, hn_ref, *, n_plain_tiles):
    j = pl.program_id(1)

    @pl.when(j == 0)
    def _():
        x = x_ref[...]
        var = jnp.mean(x * x, axis=-1, keepdims=True)
        hn_ref[...] = (x * lax.rsqrt(var + RMS_EPS) * g_ref[...]).astype(BF16)

    acc = jnp.dot(hn_ref[...], w_ref[...], preferred_element_type=F32)

    @pl.when(j < n_plain_tiles)
    def _():
        o_ref[...] = (acc * cs_ref[...]).astype(o_ref.dtype)

    @pl.when(j >= n_plain_tiles)
    def _():
        lane = lax.broadcasted_iota(I32, (1, LANES), 1)
        lo = lane < HEAD_DIM
        for c in range(acc.shape[1] // LANES):
            sl = slice(c * LANES, (c + 1) * LANES)
            a = acc[:, sl]
            s = a * a
            s_lo = jnp.sum(jnp.where(lo, s, 0.0), axis=-1, keepdims=True)
            s_hi = jnp.sum(jnp.where(lo, 0.0, s), axis=-1, keepdims=True)
            ms = jnp.where(lo, s_lo, s_hi) * (1.0 / HEAD_DIM)
            r = jnp.where(nf_ref[:, sl] > 0.0, lax.rsqrt(ms + RMS_EPS), 1.0)
            o_ref[:, sl] = ((a * r) * cs_ref[:, sl]).astype(o_ref.dtype)


def _norm_proj(x2d, gain, w_bf, col_scale, norm_flag):
    m = x2d.shape[0]
    tm = min(1024, m)
    n = w_bf.shape[1]
    n_tiles = n // PROJ_TN
    n_plain = (3 * SB_WIDTH) // PROJ_TN
    return pl.pallas_call(
        functools.partial(_norm_proj_body, n_plain_tiles=n_plain),
        grid=(m // tm, n_tiles),
        in_specs=[
            pl.BlockSpec((tm, D_MODEL), lambda i, j: (i, 0)),
            pl.BlockSpec((1, D_MODEL), lambda i, j: (0, 0)),
            pl.BlockSpec((D_MODEL, PROJ_TN), lambda i, j: (0, j)),
            pl.BlockSpec((1, PROJ_TN), lambda i, j: (0, j)),
            pl.BlockSpec((1, PROJ_TN), lambda i, j: (0, j)),
        ],
        out_specs=pl.BlockSpec((tm, PROJ_TN), lambda i, j: (i, j)),
        out_shape=jax.ShapeDtypeStruct((m, n), BF16),
        scratch_shapes=[pltpu.VMEM((tm, D_MODEL), BF16)],
        compiler_params=_cparams(("parallel", "arbitrary"), 48),
        name="norm_proj",
    )(x2d, gain, w_bf, col_scale, norm_flag)


SB_SKIP_LOG = 104.0


def _sb_body(q_ref, k_ref, v_ref, km_ref, vm_ref, o_ref, kx_ref, vx_ref, acc_ref, car_ref, *, seq):
    i = pl.program_id(2)

    @pl.when(i == 0)
    def _():
        zeros = jnp.zeros((BLOCK, LANES), BF16)
        kx_ref[0:BLOCK, :] = zeros
        vx_ref[0:BLOCK, :] = zeros
        kx_ref[BLOCK:2 * BLOCK, :] = km_ref[...]
        vx_ref[BLOCK:2 * BLOCK, :] = vm_ref[...]
        kx_ref[2 * BLOCK:, :] = k_ref[...]
        vx_ref[2 * BLOCK:, :] = v_ref[...]

    lane = lax.broadcasted_iota(I32, (BLOCK, LANES), 1)
    lo = lane < HEAD_DIM
    q = q_ref[...]
    zq = jnp.zeros_like(q)
    qm = jnp.concatenate([jnp.where(lo, q, zq), jnp.where(lo, zq, q)], axis=0)

    def later_key_matrix(nk):
        r = lax.broadcasted_iota(I32, (nk, nk), 0)
        c = lax.broadcasted_iota(I32, (nk, nk), 1)
        return (r > c).astype(F32).astype(BF16)

    tri = {BLOCK: later_key_matrix(BLOCK), SB_CHUNK: later_key_matrix(SB_CHUNK)}

    own = pl.multiple_of(BLOCK * (i + 2), BLOCK)

    def scores(start, nk, mask):
        kc = kx_ref[pl.ds(start, nk), :]
        vc = vx_ref[pl.ds(start, nk), :]
        s = lax.dot_general(qm, kc, (((1,), (1,)), ((), ())), preferred_element_type=F32)
        sp = jnp.maximum(s, 0.0) + jnp.log(1.0 + jnp.exp(-jnp.abs(s)))
        log_beta = s - sp
        valid = None
        if mask is not None:
            r_i = lax.broadcasted_iota(I32, (2 * BLOCK, nk), 0)
            c_i = lax.broadcasted_iota(I32, (2 * BLOCK, nk), 1)
            if mask == "diag":
                valid = c_i < (r_i & (BLOCK - 1))
            else:
                valid = (start + c_i) >= (BLOCK + META_PAD)
            sp = jnp.where(valid, sp, 0.0)
        inner = jnp.dot(sp.astype(BF16), tri[nk], preferred_element_type=F32)
        return log_beta - inner, jnp.sum(sp, axis=1, keepdims=True), valid, vc

    def weights(part, car):
        log_w0, _, valid, _ = part
        nk = log_w0.shape[1]
        car_k = car if nk == LANES else jnp.concatenate([car, car], axis=1)
        w = jnp.exp(log_w0 - car_k)
        if valid is not None:
            w = jnp.where(valid, w, 0.0)
        return w.astype(BF16)

    def least_of(car):
        return jnp.min(jnp.min(car, axis=0, keepdims=True))

    n_low = lax.shift_right_logical(i + 2, 1)
    diag = scores(own, BLOCK, "diag")
    low1 = scores(pl.multiple_of(own - SB_CHUNK, BLOCK), SB_CHUNK, "low")
    car = jnp.broadcast_to(diag[1], (2 * BLOCK, LANES))
    acc = jnp.dot(jnp.where(diag[2], jnp.exp(diag[0]), 0.0).astype(BF16), diag[3], preferred_element_type=F32)
    acc = acc + jnp.dot(weights(low1, car), low1[3], preferred_element_type=F32)
    car = car + low1[1]
    acc_ref[...] = acc
    car_ref[...] = car

    def step(start, mask):
        part = scores(start, SB_CHUNK, mask)
        car_ = car_ref[...]
        acc_ref[...] += jnp.dot(weights(part, car_), part[3], preferred_element_type=F32)
        new_car = car_ + part[1]
        car_ref[...] = new_car
        return least_of(new_car)

    def cond(st):
        j, least_ = st
        return (j < n_low) & (least_ <= SB_SKIP_LOG)

    def body(st):
        j, _ = st
        return j + 1, step(pl.multiple_of(own - j * SB_CHUNK, BLOCK), None)

    j_end, least = lax.while_loop(cond, body, (jnp.int32(2), least_of(car)))

    @pl.when((n_low >= 2) & (j_end == n_low) & (least <= SB_SKIP_LOG))
    def _():
        step(pl.multiple_of(own - n_low * SB_CHUNK, BLOCK), "low")

    acc = acc_ref[...]
    o_ref[...] = jnp.where(lo, acc[0:BLOCK], acc[BLOCK:2 * BLOCK])


def _sb_attention(proj, meta_blk, batch, seq):
    nq = seq // BLOCK
    n_pairs = SB_WIDTH // LANES
    return pl.pallas_call(
        functools.partial(_sb_body, seq=seq),
        grid=(batch, n_pairs, nq),
        in_specs=[
            pl.BlockSpec((BLOCK, LANES), lambda b, h, i: (b * nq + i, COL_Q_SB + h)),
            pl.BlockSpec((seq, LANES), lambda b, h, i: (b, COL_K_SB + h)),
            pl.BlockSpec((seq, LANES), lambda b, h, i: (b, COL_V_SB + h)),
            pl.BlockSpec((BLOCK, LANES), lambda b, h, i: (0, COL_K_SB + h)),
            pl.BlockSpec((BLOCK, LANES), lambda b, h, i: (0, COL_V_SB + h)),
        ],
        out_specs=pl.BlockSpec((BLOCK, LANES), lambda b, h, i: (b * nq + i, h)),
        out_shape=jax.ShapeDtypeStruct((batch * seq, SB_WIDTH), F32),
        scratch_shapes=[
            pltpu.VMEM((seq + 2 * BLOCK, LANES), BF16),
            pltpu.VMEM((seq + 2 * BLOCK, LANES), BF16),
            pltpu.VMEM((2 * BLOCK, LANES), F32),
            pltpu.VMEM((2 * BLOCK, LANES), F32),
        ],
        compiler_params=_cparams(("parallel", "parallel", "arbitrary"), 32),
        name="sb_attn",
    )(proj, proj, proj, meta_blk, meta_blk)


SWA_QB = 4


def _swa_body(q_ref, kp_ref, kc_ref, vp_ref, vc_ref, km_ref, vm_ref, bkt_ref,
              rel_ref, sink_ref, o_ref, bias_ref):
    t = pl.program_id(1)
    g = pl.program_id(2)
    h0 = 2 * t
    kv_in_hi = (lax.shift_right_logical(t, 1) & 1) == 1

    @pl.when(g == 0)
    def _():
        for tb in range(2):
            bkt = bkt_ref[tb]
            for hh in range(2):
                bias = jnp.full(bkt.shape, NEG_INF, F32)
                for b in range(NUM_BUCKETS):
                    bias = jnp.where(bkt == b, rel_ref[b, h0 + hh], bias)
                bias_ref[tb, hh * BLOCK:(hh + 1) * BLOCK, :] = bias

    lane = lax.broadcasted_iota(I32, (BLOCK, LANES), 1)
    lo = lane < HEAD_DIM
    kmask = jnp.logical_xor(lo, kv_in_hi)
    rcol = lax.broadcasted_iota(I32, (2 * BLOCK, 1), 0)
    sink = jnp.where(rcol < BLOCK, sink_ref[h0], sink_ref[h0 + 1])
    k_meta = km_ref[...]
    v_meta = vm_ref[...]

    for u in range(SWA_QB):
        rows = slice(u * BLOCK, (u + 1) * BLOCK)
        prev = slice((u - 1) * BLOCK, u * BLOCK)
        q = q_ref[rows, :]
        q_sw = jnp.concatenate([q[:, HEAD_DIM:], q[:, :HEAD_DIM]], axis=1)
        zq = jnp.zeros_like(q)
        qa = jnp.where(kmask, jnp.where(kv_in_hi, q_sw, q), zq)
        qb = jnp.where(kmask, jnp.where(kv_in_hi, q, q_sw), zq)
        qm = jnp.concatenate([qa, qb], axis=0)

        k_prev = kp_ref[...] if u == 0 else kc_ref[prev, :]
        v_prev = vp_ref[...] if u == 0 else vc_ref[prev, :]
        k_all = jnp.concatenate([k_prev, kc_ref[rows, :], k_meta], axis=0)
        v_all = jnp.concatenate([v_prev, vc_ref[rows, :], v_meta], axis=0)
        bias = bias_ref[jnp.minimum(g, 1)] if u == 0 else bias_ref[1]
        s = lax.dot_general(qm, k_all, (((1,), (1,)), ((), ())), preferred_element_type=F32) + bias

        m = jnp.maximum(jnp.max(s, axis=-1, keepdims=True), sink)
        p = jnp.exp(s - m)
        denom = jnp.sum(p, axis=-1, keepdims=True) + jnp.exp(sink - m)
        o2 = jnp.dot((p * (1.0 / denom)).astype(BF16), v_all, preferred_element_type=F32)
        oa = o2[0:BLOCK]
        ob = o2[BLOCK:2 * BLOCK]
        oa_r = pltpu.roll(oa, HEAD_DIM, 1)
        ob_r = pltpu.roll(ob, HEAD_DIM, 1)
        o_ref[rows, :] = jnp.where(lo, jnp.where(kv_in_hi, oa_r, oa), jnp.where(kv_in_hi, ob, ob_r))


def _swa_attention(proj, meta_blk, bkt, rel_bias, sinks, batch, seq):
    nq = seq // BLOCK
    ng = nq // SWA_QB
    n_pairs = SWA_WIDTH // LANES
    rows = SWA_QB * BLOCK

    def cur(base):
        return lambda b, t, g: (b * ng + g, base + t // 4)

    def prev(base):
        return lambda b, t, g: (b * nq + jnp.maximum(SWA_QB * g - 1, 0), base + t // 4)

    smem = pl.BlockSpec(memory_space=pltpu.SMEM)
    return pl.pallas_call(
        _swa_body,
        grid=(batch, n_pairs, ng),
        in_specs=[
            pl.BlockSpec((rows, LANES), lambda b, t, g: (b * ng + g, COL_Q_SW + t)),
            pl.BlockSpec((BLOCK, LANES), prev(COL_K_SW)),
            pl.BlockSpec((rows, LANES), cur(COL_K_SW)),
            pl.BlockSpec((BLOCK, LANES), prev(COL_V_SW)),
            pl.BlockSpec((rows, LANES), cur(COL_V_SW)),
            pl.BlockSpec((BLOCK, LANES), lambda b, t, g: (0, COL_K_SW + t // 4)),
            pl.BlockSpec((BLOCK, LANES), lambda b, t, g: (0, COL_V_SW + t // 4)),
            pl.BlockSpec((2, BLOCK, 3 * BLOCK), lambda b, t, g: (0, 0, 0)),
            smem,
            smem,
        ],
        out_specs=pl.BlockSpec((rows, LANES), lambda b, t, g: (b * ng + g, t)),
        out_shape=jax.ShapeDtypeStruct((batch * seq, SWA_WIDTH), F32),
        scratch_shapes=[pltpu.VMEM((2, 2 * BLOCK, 3 * BLOCK), F32)],
        compiler_params=_cparams(("parallel", "parallel", "arbitrary"), 32),
        name="swa_attn",
    )(proj, proj, proj, proj, proj, meta_blk, meta_blk, bkt, rel_bias, sinks)


def _t5_bucket_np(dist):
    max_exact = NUM_BUCKETS // 2
    d = np.maximum(dist, 0)
    ratio = np.maximum(d, 1).astype(np.float32) / np.float32(max_exact)
    large = max_exact + (np.log(ratio) / np.float32(math.log(MAX_DISTANCE / max_exact))
                         * np.float32(NUM_BUCKETS - max_exact)).astype(np.int32)
    large = np.minimum(large, NUM_BUCKETS - 1)
    return np.where(d < max_exact, d, large).astype(np.int32)


def _bucket_tables():
    ql = np.arange(BLOCK)[:, None]
    sl = np.arange(2 * BLOCK)[None, :]
    dist = ql + BLOCK - sl
    ok = (dist >= 0) & (dist < WINDOW)
    band = np.where(ok, _t5_bucket_np(dist), -1).astype(np.int32)
    band_first = np.where(sl >= BLOCK, band, -1).astype(np.int32)
    ms = np.arange(BLOCK)[None, :]
    tables = []
    for blk, b in ((0, band_first), (1, band)):
        dist_m = ql + (blk + 1) * BLOCK - ms
        meta = np.where(ms >= META_PAD, _t5_bucket_np(dist_m), -1).astype(np.int32)
        tables.append(np.concatenate([b, meta], axis=1))
    return jnp.asarray(np.stack(tables))


def _out_router_body(osb_ref, osw_ref, x_ref, gsb_ref, gsw_ref, wo_ref, lf_ref, wr_ref, br_ref,
                     h1_ref, hfp_ref, route_ref):
    def branch_norm(o_ref_, g_ref_):
        a = o_ref_[...]
        var = jnp.mean(a * a, axis=-1, keepdims=True)
        return ((a * lax.rsqrt(var + RMS_EPS)) * g_ref_[...]).astype(BF16)

    mixed = jnp.concatenate([branch_norm(osb_ref, gsb_ref), branch_norm(osw_ref, gsw_ref)], axis=1)
    h1 = x_ref[...] + jnp.dot(mixed, wo_ref[...], preferred_element_type=F32)
    h1_ref[...] = h1
    var = jnp.mean(h1 * h1, axis=-1, keepdims=True)
    hf = ((h1 * lax.rsqrt(var + RMS_EPS)) * lf_ref[...]).astype(BF16)

    half = D_MODEL // 2
    lo_bits = pltpu.bitcast(hf[:, :half].astype(F32), U32)
    hi_bits = pltpu.bitcast(hf[:, half:].astype(F32), U32)
    hfp_ref[...] = (hi_bits & jnp.uint32(0xFFFF0000)) | (lo_bits >> 16)

    logits = jnp.dot(hf, wr_ref[...], preferred_element_type=F32) + br_ref[...]
    tm = logits.shape[0]
    lane = lax.broadcasted_iota(I32, (tm, LANES), 1)
    big = jnp.int32(LANES)

    def softmax_masked(mask):
        z = jnp.where(mask, logits, NEG_INF)
        zmax = jnp.max(z, axis=-1, keepdims=True)
        e = jnp.exp(z - zmax)
        return e / jnp.sum(e, axis=-1, keepdims=True)

    def top1(p, mask):
        pm = jnp.where(mask, p, -1.0)
        v = jnp.max(pm, axis=-1, keepdims=True)
        idx = jnp.min(jnp.where(pm == v, lane, big), axis=-1, keepdims=True)
        return v, idx

    gmask = lane < N_GROUPS
    g_val, g_idx = top1(softmax_masked(gmask), gmask)
    e_lo = N_GROUPS + EXPERTS_PER_GROUP * g_idx
    emask = (lane >= e_lo) & (lane < e_lo + EXPERTS_PER_GROUP)
    e_prob = softmax_masked(emask)
    v1, i1 = top1(e_prob, emask)
    emask2 = emask & (lane != i1)
    v2, i2 = top1(e_prob, emask2)
    norm = v1 + v2
    gate1 = g_val * v1 / norm
    gate2 = g_val * v2 / norm
    ex1 = (i1 - N_GROUPS).astype(F32)
    ex2 = (i2 - N_GROUPS).astype(F32)
    route_ref[...] = jnp.where(lane == 0, gate1,
                               jnp.where(lane == 1, gate2,
                                         jnp.where(lane == 2, ex1,
                                                   jnp.where(lane == 3, ex2, 0.0))))


def _out_router(o_sb, o_sw, x2d, g_sb, g_sw, wo_bf, ln_ffn, w_router, b_router):
    m = x2d.shape[0]
    tm = TOK_TILE
    row = lambda w: pl.BlockSpec((tm, w), lambda i: (i, 0))
    const = lambda r, c: pl.BlockSpec((r, c), lambda i: (0, 0))
    return pl.pallas_call(
        _out_router_body,
        grid=(m // tm,),
        in_specs=[row(SB_WIDTH), row(SWA_WIDTH), row(D_MODEL), const(1, SB_WIDTH), const(1, SWA_WIDTH),
                  const(D_MODEL, D_MODEL), const(1, D_MODEL), const(D_MODEL, LANES), const(1, LANES)],
        out_specs=[row(D_MODEL), row(D_MODEL // 2), row(LANES)],
        out_shape=[jax.ShapeDtypeStruct((m, D_MODEL), F32),
                   jax.ShapeDtypeStruct((m, D_MODEL // 2), U32),
                   jax.ShapeDtypeStruct((m, LANES), F32)],
        compiler_params=_cparams(("parallel",), 56),
        name="out_router",
    )(o_sb, o_sw, x2d, g_sb, g_sw, wo_bf, ln_ffn, w_router, b_router)


def _experts_body(te_ref, nx_ref, ws_ref, nu_ref, src_ref, srcn_ref, hfp_ref, wg_ref, wu_ref, wd_ref, y_ref,
                  xbuf, wg_f, wu_f, wd_f, wg_bf, wu_bf, wd_bf, gsem, wsem):
    n = pl.program_id(0)
    n_used = nu_ref[0]

    def row_copy(idx_ref, r, slot):
        return pltpu.make_async_copy(hfp_ref.at[pl.ds(idx_ref[0, 0, r], 1), :],
                                     xbuf.at[slot, pl.ds(r, 1), :], gsem.at[slot])

    def weight_copies(e, slot):
        return (pltpu.make_async_copy(wg_ref.at[e], wg_f.at[slot], wsem.at[slot, 0]),
                pltpu.make_async_copy(wu_ref.at[e], wu_f.at[slot], wsem.at[slot, 1]),
                pltpu.make_async_copy(wd_ref.at[e], wd_f.at[slot], wsem.at[slot, 2]))

    @pl.when(n == 0)
    def _():
        for c in weight_copies(te_ref[0], 0):
            c.start()
        for r in range(MOE_TILE):
            row_copy(src_ref, r, 0).start()

    @pl.when(n < n_used)
    def _():
        prev = te_ref[jnp.maximum(n - 1, 0)]
        fresh = (n == 0) | (te_ref[n] != prev)

        @pl.when(fresh)
        def _():
            ws = ws_ref[n]
            for c in weight_copies(te_ref[n], ws):
                c.wait()

            @pl.when(nx_ref[n] >= 0)
            def _():
                for c in weight_copies(nx_ref[n], 1 - ws):
                    c.start()

            wg_bf[...] = wg_f[ws].astype(BF16)
            wu_bf[...] = wu_f[ws].astype(BF16)
            wd_bf[...] = wd_f[ws].astype(BF16)

        slot = n & 1
        for r in range(MOE_TILE):
            row_copy(srcn_ref, r, 1 - slot).start()
        for r in range(MOE_TILE):
            row_copy(src_ref, r, slot).wait()

        u = xbuf[slot]
        x_lo = pltpu.bitcast(u << 16, F32).astype(BF16)
        x_hi = pltpu.bitcast(u & jnp.uint32(0xFFFF0000), F32).astype(BF16)
        x = jnp.concatenate([x_lo, x_hi], axis=1)
        g = jnp.dot(x, wg_bf[...], preferred_element_type=F32)
        up = jnp.dot(x, wu_bf[...], preferred_element_type=F32)
        hdn = (g * jax.nn.sigmoid(g) * up).astype(BF16)
        y_ref[...] = jnp.dot(hdn, wd_bf[...], preferred_element_type=F32)

    @pl.when(n == n_used)
    def _():
        for r in range(MOE_TILE):
            row_copy(src_ref, r, n & 1).wait()

    @pl.when(n >= n_used)
    def _():
        y_ref[...] = jnp.zeros_like(y_ref)


def _experts(tile_expert, next_expert, w_slot, n_used, src3, hfp, w_gate, w_up, w_down):
    n_tiles = tile_expert.shape[0]
    any_space = pl.BlockSpec(memory_space=pl.ANY)
    grid_spec = pltpu.PrefetchScalarGridSpec(
        num_scalar_prefetch=4,
        grid=(n_tiles + 1,),
        in_specs=[
            pl.BlockSpec((1, 1, MOE_TILE), lambda n, *_: (n, 0, 0), memory_space=pltpu.SMEM),
            pl.BlockSpec((1, 1, MOE_TILE), lambda n, *_: (n + 1, 0, 0), memory_space=pltpu.SMEM),
            any_space, any_space, any_space, any_space,
        ],
        out_specs=pl.BlockSpec((MOE_TILE, D_MODEL), lambda n, *_: (n, 0)),
        scratch_shapes=[pltpu.VMEM((2, MOE_TILE, D_MODEL // 2), U32),
                        pltpu.VMEM((2, D_MODEL, D_EXPERT), F32),
                        pltpu.VMEM((2, D_MODEL, D_EXPERT), F32),
                        pltpu.VMEM((2, D_EXPERT, D_MODEL), F32),
                        pltpu.VMEM((D_MODEL, D_EXPERT), BF16),
                        pltpu.VMEM((D_MODEL, D_EXPERT), BF16),
                        pltpu.VMEM((D_EXPERT, D_MODEL), BF16),
                        pltpu.SemaphoreType.DMA((2,)),
                        pltpu.SemaphoreType.DMA((2, 3))],
    )
    return pl.pallas_call(
        _experts_body,
        grid_spec=grid_spec,
        out_shape=jax.ShapeDtypeStruct(((n_tiles + 1) * MOE_TILE, D_MODEL), F32),
        compiler_params=_cparams(("arbitrary",), 56),
        name="moe_experts",
    )(tile_expert, next_expert, w_slot, n_used, src3, src3, hfp, w_gate, w_up, w_down)


def _combine_body(dcur_ref, dnxt_ref, h1_ref, route_ref, y_ref, o_ref, ybuf, sem, *, n_steps):
    i = pl.program_id(0)
    n = dcur_ref.shape[2]

    def row_copy(idx_ref, a, slot):
        return pltpu.make_async_copy(y_ref.at[pl.ds(idx_ref[0, 0, a], 1), :],
                                     ybuf.at[slot, a % 2, pl.ds(a // 2, 1), :], sem.at[slot])

    @pl.when(i == 0)
    def _():
        for a in range(n):
            row_copy(dcur_ref, a, 0).start()

    @pl.when(i < n_steps)
    def _():
        slot = i & 1
        for a in range(n):
            row_copy(dnxt_ref, a, 1 - slot).start()
        for a in range(n):
            row_copy(dcur_ref, a, slot).wait()
        route = route_ref[...]
        g0 = route[:, 0:1]
        g1 = route[:, 1:2]
        o_ref[...] = h1_ref[...] + (ybuf[slot, 0] * g0 + ybuf[slot, 1] * g1)

    @pl.when(i == n_steps)
    def _():
        for a in range(n):
            row_copy(dcur_ref, a, i & 1).wait()


def _combine(dest3, h1, route, y):
    m = h1.shape[0]
    tm = dest3.shape[2] // 2
    n_steps = m // tm
    tok = lambda i: (jnp.minimum(i, n_steps - 1), 0)
    return pl.pallas_call(
        functools.partial(_combine_body, n_steps=n_steps),
        grid=(n_steps + 1,),
        in_specs=[
            pl.BlockSpec((1, 1, 2 * tm), lambda i: (i, 0, 0), memory_space=pltpu.SMEM),
            pl.BlockSpec((1, 1, 2 * tm), lambda i: (jnp.minimum(i + 1, n_steps), 0, 0), memory_space=pltpu.SMEM),
            pl.BlockSpec((tm, D_MODEL), tok),
            pl.BlockSpec((tm, LANES), tok),
            pl.BlockSpec(memory_space=pl.ANY),
        ],
        out_specs=pl.BlockSpec((tm, D_MODEL), tok),
        out_shape=jax.ShapeDtypeStruct((m, D_MODEL), F32),
        scratch_shapes=[pltpu.VMEM((2, 2, tm, D_MODEL), F32), pltpu.SemaphoreType.DMA((2,))],
        compiler_params=_cparams(("arbitrary",), 40),
        name="moe_combine",
    )(dest3, dest3, h1, route, y)


def _routing_tables(route, n_tok):
    a_expert = route[:, 2:4].astype(I32).reshape(-1)
    n_assign = a_expert.shape[0]
    eids = jnp.arange(N_EXPERTS, dtype=I32)
    sorted_e, order = lax.sort((a_expert, jnp.arange(n_assign, dtype=I32)), num_keys=1, is_stable=True)
    first = jnp.searchsorted(sorted_e, eids, side="left").astype(I32)
    counts = jnp.concatenate([first[1:], jnp.full((1,), n_assign, I32)]) - first
    padded = (counts + MOE_TILE - 1) // MOE_TILE * MOE_TILE
    seg_end = jnp.cumsum(padded)
    seg_start = seg_end - padded
    shift = seg_start - first
    shift_p = jnp.sum(jnp.where(sorted_e[None, :] == eids[:, None], shift[:, None], 0), axis=0)
    _, dest = lax.sort((order, jnp.arange(n_assign, dtype=I32) + shift_p), num_keys=1)

    n_tiles = n_assign // MOE_TILE + N_EXPERTS
    n_used = (seg_end[-1] // MOE_TILE).astype(I32)
    tile_start = jnp.arange(n_tiles + 2, dtype=I32) * MOE_TILE
    te = jnp.minimum(jnp.searchsorted(seg_end, tile_start, side="right"), N_EXPERTS - 1).astype(I32)
    used = counts > 0
    later_used = used[None, :] & (eids[None, :] > eids[:, None])
    next_e = jnp.min(jnp.where(later_used, eids[None, :], N_EXPERTS), axis=1)
    next_e = jnp.where(next_e == N_EXPERTS, -1, next_e).astype(I32)
    ordinal = jnp.cumsum(used.astype(I32)) - 1
    off = tile_start - seg_start[te]
    base = jnp.clip(first[te] + off, 0, n_assign)
    live = jnp.where(jnp.arange(n_tiles + 2) < n_used, jnp.clip(counts[te] - off, 0, MOE_TILE), 0)
    tok_sorted = jnp.concatenate([order // 2, jnp.zeros((MOE_TILE,), I32)])
    window = jax.vmap(lambda b: lax.dynamic_slice(tok_sorted, (b,), (MOE_TILE,)))(base)
    src = jnp.where(jnp.arange(MOE_TILE, dtype=I32)[None, :] < live[:, None], window, 0)
    return (dest.astype(I32), src.astype(I32), te[:n_tiles], next_e[te[:n_tiles]],
            (ordinal[te[:n_tiles]] & 1).astype(I32), n_used.reshape(1))


def kernel(x, meta_tokens, rel_bias, ln_mix, w_in, q_norm, k_norm, sinks, out_norm_sb, out_norm_swa,
           w_out, ln_ffn, w_router_group, b_router_group, w_router_expert, b_router_expert,
           w_gate, w_up, w_down):
    batch, seq, _ = x.shape
    n_tok = batch * seq
    x2d = x.reshape(n_tok, D_MODEL)

    ones = lambda n: jnp.ones((n,), F32)
    col_scale = jnp.concatenate([
        ones(SB_WIDTH) * SCALE, ones(2 * SB_WIDTH),
        jnp.tile(q_norm[0], SWA_Q_HEADS) * SCALE, jnp.tile(k_norm[0], SWA_KV_HEADS), ones(SWA_KV_WIDTH)])[None]
    norm_flag = jnp.concatenate([jnp.zeros((3 * SB_WIDTH,), F32), ones(SWA_WIDTH + SWA_KV_WIDTH),
                                 jnp.zeros((SWA_KV_WIDTH,), F32)])[None]
    w_in_bf = w_in[0].astype(BF16)
    gain_mix = ln_mix[0][None]

    proj = _norm_proj(x2d, gain_mix, w_in_bf, col_scale, norm_flag)
    meta_proj = _norm_proj(meta_tokens.astype(F32), gain_mix, w_in_bf, col_scale, norm_flag)
    meta_blk = jnp.concatenate([jnp.zeros((META_PAD, IN_PROJ_WIDTH), BF16), meta_proj], axis=0)

    o_sb = _sb_attention(proj, meta_blk, batch, seq)
    o_sw = _swa_attention(proj, meta_blk, _bucket_tables(), rel_bias.astype(F32), sinks[0].astype(F32),
                          batch, seq)

    w_router = jnp.concatenate(
        [w_router_group[0], w_router_expert[0],
         jnp.zeros((D_MODEL, LANES - N_GROUPS - N_EXPERTS), F32)], axis=1).astype(BF16)
    b_router = jnp.concatenate(
        [b_router_group[0], b_router_expert[0], jnp.zeros((LANES - N_GROUPS - N_EXPERTS,), F32)])[None]
    h1, hfp, route = _out_router(o_sb, o_sw, x2d, out_norm_sb[0][None], out_norm_swa[0][None],
                                 w_out[0].astype(BF16), ln_ffn[0][None], w_router, b_router)

    dest, src, tile_expert, next_expert, w_slot, n_used = _routing_tables(route, n_tok)
    y = _experts(tile_expert, next_expert, w_slot, n_used, src.reshape(-1, 1, MOE_TILE), hfp,
                 w_gate[0], w_up[0], w_down[0])
    dest3 = jnp.concatenate([dest, jnp.zeros((2 * TOK_TILE,), I32)]).reshape(-1, 1, 2 * TOK_TILE)
    out = _combine(dest3, h1, route, y)
    return out.reshape(batch, seq, D_MODEL)
```

```python
import functools
import math

import numpy as np
import jax
import jax.numpy as jnp
from jax import lax
from jax.experimental import pallas as pl
from jax.experimental.pallas import tpu as pltpu

F32 = jnp.float32
BF16 = jnp.bfloat16
I32 = jnp.int32
U32 = jnp.uint32

D_MODEL = 2048
HEAD_DIM = 64
LANES = 128
SB_HEADS = 16
SWA_Q_HEADS = 16
SWA_KV_HEADS = 4
SB_WIDTH = SB_HEADS * HEAD_DIM
SWA_WIDTH = SWA_Q_HEADS * HEAD_DIM
SWA_KV_WIDTH = SWA_KV_HEADS * HEAD_DIM
IN_PROJ_WIDTH = 3 * SB_WIDTH + SWA_WIDTH + 2 * SWA_KV_WIDTH
BLOCK = 128
WINDOW = 128
N_META = 16
META_PAD = BLOCK - N_META
NUM_BUCKETS = 32
MAX_DISTANCE = 128
N_GROUPS = 8
EXPERTS_PER_GROUP = 8
N_EXPERTS = N_GROUPS * EXPERTS_PER_GROUP
D_EXPERT = D_MODEL // 4
RMS_EPS = 1e-6
SCALE = HEAD_DIM ** -0.5

COL_Q_SB = 0
COL_K_SB = SB_WIDTH // LANES
COL_V_SB = 2 * SB_WIDTH // LANES
COL_Q_SW = 3 * SB_WIDTH // LANES
COL_K_SW = COL_Q_SW + SWA_WIDTH // LANES
COL_V_SW = COL_K_SW + SWA_KV_WIDTH // LANES

PROJ_TN = 768
SB_CHUNK = 256
MOE_TILE = 256
TOK_TILE = 256
NEG_INF = float("-inf")


def _cparams(sem, vmem_mb):
    return pltpu.CompilerParams(dimension_semantics=sem, vmem_limit_bytes=vmem_mb * 1024 * 1024)


def _norm_proj_body(x_ref, g_ref, w_ref, cs_ref, nf_ref, o_ref, hn_ref, *, n_plain_tiles):
    j = pl.program_id(1)

    @pl.when(j == 0)
    def _():
        x = x_ref[...]
        var = jnp.mean(x * x, axis=-1, keepdims=True)
        hn_ref[...] = (x * lax.rsqrt(var + RMS_EPS) * g_ref[...]).astype(BF16)

    acc = jnp.dot(hn_ref[...], w_ref[...], preferred_element_type=F32)

    @pl.when(j < n_plain_tiles)
    def _():
        o_ref[...] = (acc * cs_ref[...]).astype(o_ref.dtype)

    @pl.when(j >= n_plain_tiles)
    def _():
        lane = lax.broadcasted_iota(I32, (1, LANES), 1)
        lo = lane < HEAD_DIM
        for c in range(acc.shape[1] // LANES):
            sl = slice(c * LANES, (c + 1) * LANES)
            a = acc[:, sl]
            s = a * a
            s_lo = jnp.sum(jnp.where(lo, s, 0.0), axis=-1, keepdims=True)
            s_hi = jnp.sum(jnp.where(lo, 0.0, s), axis=-1, keepdims=True)
            ms = jnp.where(lo, s_lo, s_hi) * (1.0 / HEAD_DIM)
            r = jnp.where(nf_ref[:, sl] > 0.0, lax.rsqrt(ms + RMS_EPS), 1.0)
            o_ref[:, sl] = ((a * r) * cs_ref[:, sl]).astype(o_ref.dtype)


def _norm_proj(x2d, gain, w_bf, col_scale, norm_flag):
    m = x2d.shape[0]
    tm = min(1024, m)
    n = w_bf.shape[1]
    n_tiles = n // PROJ_TN
    n_plain = (3 * SB_WIDTH) // PROJ_TN
    return pl.pallas_call(
        functools.partial(_norm_proj_body, n_plain_tiles=n_plain),
        grid=(m // tm, n_tiles),
        in_specs=[
            pl.BlockSpec((tm, D_MODEL), lambda i, j: (i, 0)),
            pl.BlockSpec((1, D_MODEL), lambda i, j: (0, 0)),
            pl.BlockSpec((D_MODEL, PROJ_TN), lambda i, j: (0, j)),
            pl.BlockSpec((1, PROJ_TN), lambda i, j: (0, j)),
            pl.BlockSpec((1, PROJ_TN), lambda i, j: (0, j)),
        ],
        out_specs=pl.BlockSpec((tm, PROJ_TN), lambda i, j: (i, j)),
        out_shape=jax.ShapeDtypeStruct((m, n), BF16),
        scratch_shapes=[pltpu.VMEM((tm, D_MODEL), BF16)],
        compiler_params=_cparams(("parallel", "arbitrary"), 48),
        name="norm_proj",
    )(x2d, gain, w_bf, col_scale, norm_flag)


SB_SKIP_LOG = 104.0


SB_QB = 4


def _sb_body(q_ref, k_ref, v_ref, km_ref, vm_ref, o_ref, kx_ref, vx_ref, acc_ref, car_ref, *, seq):
    g = pl.program_id(2)

    @pl.when(g == 0)
    def _():
        zeros = jnp.zeros((BLOCK, LANES), BF16)
        kx_ref[0:BLOCK, :] = zeros
        vx_ref[0:BLOCK, :] = zeros
        kx_ref[BLOCK:2 * BLOCK, :] = km_ref[...]
        vx_ref[BLOCK:2 * BLOCK, :] = vm_ref[...]
        kx_ref[2 * BLOCK:, :] = k_ref[...]
        vx_ref[2 * BLOCK:, :] = v_ref[...]

    lane = lax.broadcasted_iota(I32, (BLOCK, LANES), 1)
    lo = lane < HEAD_DIM

    def later_key_matrix(nk):
        r = lax.broadcasted_iota(I32, (nk, nk), 0)
        c = lax.broadcasted_iota(I32, (nk, nk), 1)
        return (r > c).astype(F32).astype(BF16)

    tri = {BLOCK: later_key_matrix(BLOCK), SB_CHUNK: later_key_matrix(SB_CHUNK)}

    def scores(qm, start, nk, mask):
        kc = kx_ref[pl.ds(start, nk), :]
        vc = vx_ref[pl.ds(start, nk), :]
        s = lax.dot_general(qm, kc, (((1,), (1,)), ((), ())), preferred_element_type=F32)
        sp = jnp.maximum(s, 0.0) + jnp.log(1.0 + jnp.exp(-jnp.abs(s)))
        log_beta = s - sp
        valid = None
        if mask is not None:
            r_i = lax.broadcasted_iota(I32, (2 * BLOCK, nk), 0)
            c_i = lax.broadcasted_iota(I32, (2 * BLOCK, nk), 1)
            if mask == "diag":
                valid = c_i < (r_i & (BLOCK - 1))
            else:
                valid = (start + c_i) >= (BLOCK + META_PAD)
            sp = jnp.where(valid, sp, 0.0)
        inner = jnp.dot(sp.astype(BF16), tri[nk], preferred_element_type=F32)
        return log_beta - inner, jnp.sum(sp, axis=1, keepdims=True), valid, vc

    def weights(part, car):
        log_w0, _, valid, _ = part
        nk = log_w0.shape[1]
        car_k = car if nk == LANES else jnp.concatenate([car, car], axis=1)
        w = jnp.exp(log_w0 - car_k)
        if valid is not None:
            w = jnp.where(valid, w, 0.0)
        return w.astype(BF16)

    def least_of(car):
        return jnp.min(jnp.min(car, axis=0, keepdims=True))

    tiles = []
    for u in range(SB_QB):
        i = SB_QB * g + u
        own = pl.multiple_of(BLOCK * (i + 2), BLOCK)
        q = q_ref[u * BLOCK:(u + 1) * BLOCK, :]
        zq = jnp.zeros_like(q)
        qm = jnp.concatenate([jnp.where(lo, q, zq), jnp.where(lo, zq, q)], axis=0)
        diag = scores(qm, own, BLOCK, "diag")
        low1 = scores(qm, pl.multiple_of(own - SB_CHUNK, BLOCK), SB_CHUNK, "low")
        car = jnp.broadcast_to(diag[1], (2 * BLOCK, LANES))
        acc = jnp.dot(jnp.where(diag[2], jnp.exp(diag[0]), 0.0).astype(BF16), diag[3],
                      preferred_element_type=F32)
        acc = acc + jnp.dot(weights(low1, car), low1[3], preferred_element_type=F32)
        car = car + low1[1]
        acc_ref[u] = acc
        car_ref[u] = car
        tiles.append((i, own, qm, least_of(car)))

    for u, (i, own, qm, least0) in enumerate(tiles):
        n_low = lax.shift_right_logical(i + 2, 1)

        def step(start, mask, u=u, qm=qm):
            part = scores(qm, start, SB_CHUNK, mask)
            car_ = car_ref[u]
            acc_ref[u] += jnp.dot(weights(part, car_), part[3], preferred_element_type=F32)
            new_car = car_ + part[1]
            car_ref[u] = new_car
            return least_of(new_car)

        def cond(st, n_low=n_low):
            j, least_ = st
            return (j < n_low) & (least_ <= SB_SKIP_LOG)

        def body(st, own=own, step=step):
            j, _ = st
            return j + 1, step(pl.multiple_of(own - j * SB_CHUNK, BLOCK), None)

        j_end, least = lax.while_loop(cond, body, (jnp.int32(2), least0))

        @pl.when((n_low >= 2) & (j_end == n_low) & (least <= SB_SKIP_LOG))
        def _(own=own, n_low=n_low, step=step):
            step(pl.multiple_of(own - n_low * SB_CHUNK, BLOCK), "low")

        acc = acc_ref[u]
        o_ref[u * BLOCK:(u + 1) * BLOCK, :] = jnp.where(lo, acc[0:BLOCK], acc[BLOCK:2 * BLOCK])


def _sb_attention(proj, meta_blk, batch, seq):
    nq = seq // BLOCK
    ng = nq // SB_QB
    n_pairs = SB_WIDTH // LANES
    rows = SB_QB * BLOCK
    return pl.pallas_call(
        functools.partial(_sb_body, seq=seq),
        grid=(batch, n_pairs, ng),
        in_specs=[
            pl.BlockSpec((rows, LANES), lambda b, h, g: (b * ng + g, COL_Q_SB + h)),
            pl.BlockSpec((seq, LANES), lambda b, h, g: (b, COL_K_SB + h)),
            pl.BlockSpec((seq, LANES), lambda b, h, g: (b, COL_V_SB + h)),
            pl.BlockSpec((BLOCK, LANES), lambda b, h, g: (0, COL_K_SB + h)),
            pl.BlockSpec((BLOCK, LANES), lambda b, h, g: (0, COL_V_SB + h)),
        ],
        out_specs=pl.BlockSpec((rows, LANES), lambda b, h, g: (b * ng + g, h)),
        out_shape=jax.ShapeDtypeStruct((batch * seq, SB_WIDTH), F32),
        scratch_shapes=[
            pltpu.VMEM((seq + 2 * BLOCK, LANES), BF16),
            pltpu.VMEM((seq + 2 * BLOCK, LANES), BF16),
            pltpu.VMEM((SB_QB, 2 * BLOCK, LANES), F32),
            pltpu.VMEM((SB_QB, 2 * BLOCK, LANES), F32),
        ],
        compiler_params=_cparams(("parallel", "parallel", "arbitrary"), 32),
        name="sb_attn",
    )(proj, proj, proj, meta_blk, meta_blk)


SWA_QB = 4


def _swa_body(q_ref, kp_ref, kc_ref, vp_ref, vc_ref, km_ref, vm_ref, bkt_ref,
              rel_ref, sink_ref, o_ref, bias_ref):
    t = pl.program_id(1)
    g = pl.program_id(2)
    h0 = 2 * t
    kv_in_hi = (lax.shift_right_logical(t, 1) & 1) == 1

    @pl.when(g == 0)
    def _():
        for tb in range(2):
            bkt = bkt_ref[tb]
            for hh in range(2):
                bias = jnp.full(bkt.shape, NEG_INF, F32)
                for b in range(NUM_BUCKETS):
                    bias = jnp.where(bkt == b, rel_ref[b, h0 + hh], bias)
                bias_ref[tb, hh * BLOCK:(hh + 1) * BLOCK, :] = bias

    lane = lax.broadcasted_iota(I32, (BLOCK, LANES), 1)
    lo = lane < HEAD_DIM
    kmask = jnp.logical_xor(lo, kv_in_hi)
    rcol = lax.broadcasted_iota(I32, (2 * BLOCK, 1), 0)
    sink = jnp.where(rcol < BLOCK, sink_ref[h0], sink_ref[h0 + 1])
    k_meta = km_ref[...]
    v_meta = vm_ref[...]

    for u in range(SWA_QB):
        rows = slice(u * BLOCK, (u + 1) * BLOCK)
        prev = slice((u - 1) * BLOCK, u * BLOCK)
        q = q_ref[rows, :]
        q_sw = jnp.concatenate([q[:, HEAD_DIM:], q[:, :HEAD_DIM]], axis=1)
        zq = jnp.zeros_like(q)
        qa = jnp.where(kmask, jnp.where(kv_in_hi, q_sw, q), zq)
        qb = jnp.where(kmask, jnp.where(kv_in_hi, q, q_sw), zq)
        qm = jnp.concatenate([qa, qb], axis=0)

        k_prev = kp_ref[...] if u == 0 else kc_ref[prev, :]
        v_prev = vp_ref[...] if u == 0 else vc_ref[prev, :]
        k_all = jnp.concatenate([k_prev, kc_ref[rows, :], k_meta], axis=0)
        v_all = jnp.concatenate([v_prev, vc_ref[rows, :], v_meta], axis=0)
        bias = bias_ref[jnp.minimum(g, 1)] if u == 0 else bias_ref[1]
        s = lax.dot_general(qm, k_all, (((1,), (1,)), ((), ())), preferred_element_type=F32) + bias

        m = jnp.maximum(jnp.max(s, axis=-1, keepdims=True), sink)
        p = jnp.exp(s - m)
        denom = jnp.sum(p, axis=-1, keepdims=True) + jnp.exp(sink - m)
        o2 = jnp.dot((p * (1.0 / denom)).astype(BF16), v_all, preferred_element_type=F32)
        oa = o2[0:BLOCK]
        ob = o2[BLOCK:2 * BLOCK]
        oa_r = pltpu.roll(oa, HEAD_DIM, 1)
        ob_r = pltpu.roll(ob, HEAD_DIM, 1)
        o_ref[rows, :] = jnp.where(lo, jnp.where(kv_in_hi, oa_r, oa), jnp.where(kv_in_hi, ob, ob_r))


def _swa_attention(proj, meta_blk, bkt, rel_bias, sinks, batch, seq):
    nq = seq // BLOCK
    ng = nq // SWA_QB
    n_pairs = SWA_WIDTH // LANES
    rows = SWA_QB * BLOCK

    def cur(base):
        return lambda b, t, g: (b * ng + g, base + t // 4)

    def prev(base):
        return lambda b, t, g: (b * nq + jnp.maximum(SWA_QB * g - 1, 0), base + t // 4)

    smem = pl.BlockSpec(memory_space=pltpu.SMEM)
    return pl.pallas_call(
        _swa_body,
        grid=(batch, n_pairs, ng),
        in_specs=[
            pl.BlockSpec((rows, LANES), lambda b, t, g: (b * ng + g, COL_Q_SW + t)),
            pl.BlockSpec((BLOCK, LANES), prev(COL_K_SW)),
            pl.BlockSpec((rows, LANES), cur(COL_K_SW)),
            pl.BlockSpec((BLOCK, LANES), prev(COL_V_SW)),
            pl.BlockSpec((rows, LANES), cur(COL_V_SW)),
            pl.BlockSpec((BLOCK, LANES), lambda b, t, g: (0, COL_K_SW + t // 4)),
            pl.BlockSpec((BLOCK, LANES), lambda b, t, g: (0, COL_V_SW + t // 4)),
            pl.BlockSpec((2, BLOCK, 3 * BLOCK), lambda b, t, g: (0, 0, 0)),
            smem,
            smem,
        ],
        out_specs=pl.BlockSpec((rows, LANES), lambda b, t, g: (b * ng + g, t)),
        out_shape=jax.ShapeDtypeStruct((batch * seq, SWA_WIDTH), F32),
        scratch_shapes=[pltpu.VMEM((2, 2 * BLOCK, 3 * BLOCK), F32)],
        compiler_params=_cparams(("parallel", "parallel", "arbitrary"), 32),
        name="swa_attn",
    )(proj, proj, proj, proj, proj, meta_blk, meta_blk, bkt, rel_bias, sinks)


def _t5_bucket_np(dist):
    max_exact = NUM_BUCKETS // 2
    d = np.maximum(dist, 0)
    ratio = np.maximum(d, 1).astype(np.float32) / np.float32(max_exact)
    large = max_exact + (np.log(ratio) / np.float32(math.log(MAX_DISTANCE / max_exact))
                         * np.float32(NUM_BUCKETS - max_exact)).astype(np.int32)
    large = np.minimum(large, NUM_BUCKETS - 1)
    return np.where(d < max_exact, d, large).astype(np.int32)


def _bucket_tables():
    ql = np.arange(BLOCK)[:, None]
    sl = np.arange(2 * BLOCK)[None, :]
    dist = ql + BLOCK - sl
    ok = (dist >= 0) & (dist < WINDOW)
    band = np.where(ok, _t5_bucket_np(dist), -1).astype(np.int32)
    band_first = np.where(sl >= BLOCK, band, -1).astype(np.int32)
    ms = np.arange(BLOCK)[None, :]
    tables = []
    for blk, b in ((0, band_first), (1, band)):
        dist_m = ql + (blk + 1) * BLOCK - ms
        meta = np.where(ms >= META_PAD, _t5_bucket_np(dist_m), -1).astype(np.int32)
        tables.append(np.concatenate([b, meta], axis=1))
    return jnp.asarray(np.stack(tables))


def _out_router_body(osb_ref, osw_ref, x_ref, gsb_ref, gsw_ref, wo_ref, lf_ref, wr_ref, br_ref,
                     h1_ref, hfp_ref, route_ref):
    def branch_norm(o_ref_, g_ref_):
        a = o_ref_[...]
        var = jnp.mean(a * a, axis=-1, keepdims=True)
        return ((a * lax.rsqrt(var + RMS_EPS)) * g_ref_[...]).astype(BF16)

    mixed = jnp.concatenate([branch_norm(osb_ref, gsb_ref), branch_norm(osw_ref, gsw_ref)], axis=1)
    h1 = x_ref[...] + jnp.dot(mixed, wo_ref[...], preferred_element_type=F32)
    h1_ref[...] = h1
    var = jnp.mean(h1 * h1, axis=-1, keepdims=True)
    hf = ((h1 * lax.rsqrt(var + RMS_EPS)) * lf_ref[...]).astype(BF16)

    half = D_MODEL // 2
    lo_bits = pltpu.bitcast(hf[:, :half].astype(F32), U32)
    hi_bits = pltpu.bitcast(hf[:, half:].astype(F32), U32)
    hfp_ref[...] = (hi_bits & jnp.uint32(0xFFFF0000)) | (lo_bits >> 16)

    logits = jnp.dot(hf, wr_ref[...], preferred_element_type=F32) + br_ref[...]
    tm = logits.shape[0]
    lane = lax.broadcasted_iota(I32, (tm, LANES), 1)
    big = jnp.int32(LANES)

    def softmax_masked(mask):
        z = jnp.where(mask, logits, NEG_INF)
        zmax = jnp.max(z, axis=-1, keepdims=True)
        e = jnp.exp(z - zmax)
        return e / jnp.sum(e, axis=-1, keepdims=True)

    def top1(p, mask):
        pm = jnp.where(mask, p, -1.0)
        v = jnp.max(pm, axis=-1, keepdims=True)
        idx = jnp.min(jnp.where(pm == v, lane, big), axis=-1, keepdims=True)
        return v, idx

    gmask = lane < N_GROUPS
    g_val, g_idx = top1(softmax_masked(gmask), gmask)
    e_lo = N_GROUPS + EXPERTS_PER_GROUP * g_idx
    emask = (lane >= e_lo) & (lane < e_lo + EXPERTS_PER_GROUP)
    e_prob = softmax_masked(emask)
    v1, i1 = top1(e_prob, emask)
    emask2 = emask & (lane != i1)
    v2, i2 = top1(e_prob, emask2)
    norm = v1 + v2
    gate1 = g_val * v1 / norm
    gate2 = g_val * v2 / norm
    ex1 = (i1 - N_GROUPS).astype(F32)
    ex2 = (i2 - N_GROUPS).astype(F32)
    route_ref[...] = jnp.where(lane == 0, gate1,
                               jnp.where(lane == 1, gate2,
                                         jnp.where(lane == 2, ex1,
                                                   jnp.where(lane == 3, ex2, 0.0))))


def _out_router(o_sb, o_sw, x2d, g_sb, g_sw, wo_bf, ln_ffn, w_router, b_router):
    m = x2d.shape[0]
    tm = TOK_TILE
    row = lambda w: pl.BlockSpec((tm, w), lambda i: (i, 0))
    const = lambda r, c: pl.BlockSpec((r, c), lambda i: (0, 0))
    return pl.pallas_call(
        _out_router_body,
        grid=(m // tm,),
        in_specs=[row(SB_WIDTH), row(SWA_WIDTH), row(D_MODEL), const(1, SB_WIDTH), const(1, SWA_WIDTH),
                  const(D_MODEL, D_MODEL), const(1, D_MODEL), const(D_MODEL, LANES), const(1, LANES)],
        out_specs=[row(D_MODEL), row(D_MODEL // 2), row(LANES)],
        out_shape=[jax.ShapeDtypeStruct((m, D_MODEL), F32),
                   jax.ShapeDtypeStruct((m, D_MODEL // 2), U32),
                   jax.ShapeDtypeStruct((m, LANES), F32)],
        compiler_params=_cparams(("parallel",), 56),
        name="out_router",
    )(o_sb, o_sw, x2d, g_sb, g_sw, wo_bf, ln_ffn, w_router, b_router)


W_DMA_PRIORITY = 1


def _experts_body(te_ref, nx_ref, ws_ref, base_ref, nu_ref, tok_ref, hfp_ref, wg_ref, wu_ref, wd_ref, y_ref,
                  xbuf, wg_f, wu_f, wd_f, wg_bf, wu_bf, wd_bf, gsem, wsem):
    n = pl.program_id(0)
    n_used = nu_ref[0]

    def row_copy(tile, r, slot):
        tok = tok_ref[base_ref[tile] + r]
        return pltpu.make_async_copy(hfp_ref.at[pl.ds(tok, 1), :], xbuf.at[slot, pl.ds(r, 1), :], gsem.at[slot])

    def row_wait(r, slot):
        pltpu.make_async_copy(hfp_ref.at[pl.ds(0, 1), :], xbuf.at[slot, pl.ds(r, 1), :], gsem.at[slot]).wait()

    def weight_copies(e, slot):
        return (pltpu.make_async_copy(wg_ref.at[e], wg_f.at[slot], wsem.at[slot, 0]),
                pltpu.make_async_copy(wu_ref.at[e], wu_f.at[slot], wsem.at[slot, 1]),
                pltpu.make_async_copy(wd_ref.at[e], wd_f.at[slot], wsem.at[slot, 2]))

    @pl.when(n == 0)
    def _():
        for c in weight_copies(te_ref[0], 0):
            c.start(priority=W_DMA_PRIORITY)
        for r in range(MOE_TILE):
            row_copy(0, r, 0).start()

    @pl.when(n < n_used)
    def _():
        prev = te_ref[jnp.maximum(n - 1, 0)]
        fresh = (n == 0) | (te_ref[n] != prev)

        @pl.when(fresh)
        def _():
            ws = ws_ref[n]
            for c in weight_copies(te_ref[n], ws):
                c.wait()

            @pl.when(nx_ref[n] >= 0)
            def _():
                for c in weight_copies(nx_ref[n], 1 - ws):
                    c.start(priority=W_DMA_PRIORITY)

            wg_bf[...] = wg_f[ws].astype(BF16)
            wu_bf[...] = wu_f[ws].astype(BF16)
            wd_bf[...] = wd_f[ws].astype(BF16)

        slot = n & 1
        for r in range(MOE_TILE):
            row_copy(n + 1, r, 1 - slot).start()
        for r in range(MOE_TILE):
            row_wait(r, slot)

        u = xbuf[slot]
        x_lo = pltpu.bitcast(u << 16, F32).astype(BF16)
        x_hi = pltpu.bitcast(u & jnp.uint32(0xFFFF0000), F32).astype(BF16)
        x = jnp.concatenate([x_lo, x_hi], axis=1)
        g = jnp.dot(x, wg_bf[...], preferred_element_type=F32)
        up = jnp.dot(x, wu_bf[...], preferred_element_type=F32)
        hdn = (g * jax.nn.sigmoid(g) * up).astype(BF16)
        y_ref[...] = jnp.dot(hdn, wd_bf[...], preferred_element_type=F32)

    @pl.when(n == n_used)
    def _():
        for r in range(MOE_TILE):
            row_wait(r, n & 1)

    @pl.when(n >= n_used)
    def _():
        y_ref[...] = jnp.zeros_like(y_ref)


def _experts(tile_expert, next_expert, w_slot, tile_base, n_used, tok_sorted, hfp, w_gate, w_up, w_down):
    n_tiles = tile_expert.shape[0]
    any_space = pl.BlockSpec(memory_space=pl.ANY)
    grid_spec = pltpu.PrefetchScalarGridSpec(
        num_scalar_prefetch=6,
        grid=(n_tiles + 1,),
        in_specs=[any_space, any_space, any_space, any_space],
        out_specs=pl.BlockSpec((MOE_TILE, D_MODEL), lambda n, *_: (n, 0)),
        scratch_shapes=[pltpu.VMEM((2, MOE_TILE, D_MODEL // 2), U32),
                        pltpu.VMEM((2, D_MODEL, D_EXPERT), F32),
                        pltpu.VMEM((2, D_MODEL, D_EXPERT), F32),
                        pltpu.VMEM((2, D_EXPERT, D_MODEL), F32),
                        pltpu.VMEM((D_MODEL, D_EXPERT), BF16),
                        pltpu.VMEM((D_MODEL, D_EXPERT), BF16),
                        pltpu.VMEM((D_EXPERT, D_MODEL), BF16),
                        pltpu.SemaphoreType.DMA((2,)),
                        pltpu.SemaphoreType.DMA((2, 3))],
    )
    return pl.pallas_call(
        _experts_body,
        grid_spec=grid_spec,
        out_shape=jax.ShapeDtypeStruct(((n_tiles + 1) * MOE_TILE, D_MODEL), F32),
        compiler_params=_cparams(("arbitrary",), 56),
        name="moe_experts",
    )(tile_expert, next_expert, w_slot, tile_base, n_used, tok_sorted, hfp, w_gate, w_up, w_down)


def _combine_body(dcur_ref, dnxt_ref, h1_ref, route_ref, y_ref, o_ref, ybuf, sem, *, n_steps):
    i = pl.program_id(0)
    n = dcur_ref.shape[2]

    def row_copy(idx_ref, a, slot):
        return pltpu.make_async_copy(y_ref.at[pl.ds(idx_ref[0, 0, a], 1), :],
                                     ybuf.at[slot, a % 2, pl.ds(a // 2, 1), :], sem.at[slot])

    @pl.when(i == 0)
    def _():
        for a in range(n):
            row_copy(dcur_ref, a, 0).start()

    @pl.when(i < n_steps)
    def _():
        slot = i & 1
        for a in range(n):
            row_copy(dnxt_ref, a, 1 - slot).start(priority=a % 2)
        for a in range(n):
            row_copy(dcur_ref, a, slot).wait()
        route = route_ref[...]
        g0 = route[:, 0:1]
        g1 = route[:, 1:2]
        o_ref[...] = h1_ref[...] + (ybuf[slot, 0] * g0 + ybuf[slot, 1] * g1)

    @pl.when(i == n_steps)
    def _():
        for a in range(n):
            row_copy(dcur_ref, a, i & 1).wait()


def _combine(dest3, h1, route, y):
    m = h1.shape[0]
    tm = dest3.shape[2] // 2
    n_steps = m // tm
    tok = lambda i: (jnp.minimum(i, n_steps - 1), 0)
    return pl.pallas_call(
        functools.partial(_combine_body, n_steps=n_steps),
        grid=(n_steps + 1,),
        in_specs=[
            pl.BlockSpec((1, 1, 2 * tm), lambda i: (i, 0, 0), memory_space=pltpu.SMEM),
            pl.BlockSpec((1, 1, 2 * tm), lambda i: (jnp.minimum(i + 1, n_steps), 0, 0), memory_space=pltpu.SMEM),
            pl.BlockSpec((tm, D_MODEL), tok),
            pl.BlockSpec((tm, LANES), tok),
            pl.BlockSpec(memory_space=pl.ANY),
        ],
        out_specs=pl.BlockSpec((tm, D_MODEL), tok),
        out_shape=jax.ShapeDtypeStruct((m, D_MODEL), F32),
        scratch_shapes=[pltpu.VMEM((2, 2, tm, D_MODEL), F32), pltpu.SemaphoreType.DMA((2,))],
        compiler_params=_cparams(("arbitrary",), 40),
        name="moe_combine",
    )(dest3, dest3, h1, route, y)


def _routing_tables(route, n_tok):
    a_expert = route[:, 2:4].astype(I32).reshape(-1)
    n_assign = a_expert.shape[0]
    eids = jnp.arange(N_EXPERTS, dtype=I32)
    sorted_e, order = lax.sort((a_expert, jnp.arange(n_assign, dtype=I32)), num_keys=1, is_stable=True)
    first = jnp.searchsorted(sorted_e, eids, side="left").astype(I32)
    counts = jnp.concatenate([first[1:], jnp.full((1,), n_assign, I32)]) - first
    padded = (counts + MOE_TILE - 1) // MOE_TILE * MOE_TILE
    seg_end = jnp.cumsum(padded)
    seg_start = seg_end - padded
    shift = seg_start - first
    shift_p = jnp.sum(jnp.where(sorted_e[None, :] == eids[:, None], shift[:, None], 0), axis=0)
    _, dest = lax.sort((order, jnp.arange(n_assign, dtype=I32) + shift_p), num_keys=1)

    n_tiles = n_assign // MOE_TILE + N_EXPERTS
    n_used = (seg_end[-1] // MOE_TILE).astype(I32)
    tile_start = jnp.arange(n_tiles + 2, dtype=I32) * MOE_TILE
    te = jnp.minimum(jnp.searchsorted(seg_end, tile_start, side="right"), N_EXPERTS - 1).astype(I32)
    used = counts > 0
    later_used = used[None, :] & (eids[None, :] > eids[:, None])
    next_e = jnp.min(jnp.where(later_used, eids[None, :], N_EXPERTS), axis=1)
    next_e = jnp.where(next_e == N_EXPERTS, -1, next_e).astype(I32)
    ordinal = jnp.cumsum(used.astype(I32)) - 1
    base = first[te] + tile_start - seg_start[te]
    base = jnp.where(jnp.arange(n_tiles + 2) < n_used, jnp.clip(base, 0, n_assign), n_assign).astype(I32)
    tok_sorted = jnp.concatenate([order // 2, jnp.zeros((MOE_TILE,), I32)])
    return (dest.astype(I32), tok_sorted, base, te[:n_tiles], next_e[te[:n_tiles]],
            (ordinal[te[:n_tiles]] & 1).astype(I32), n_used.reshape(1))


def kernel(x, meta_tokens, rel_bias, ln_mix, w_in, q_norm, k_norm, sinks, out_norm_sb, out_norm_swa,
           w_out, ln_ffn, w_router_group, b_router_group, w_router_expert, b_router_expert,
           w_gate, w_up, w_down):
    batch, seq, _ = x.shape
    n_tok = batch * seq
    x2d = x.reshape(n_tok, D_MODEL)

    ones = lambda n: jnp.ones((n,), F32)
    col_scale = jnp.concatenate([
        ones(SB_WIDTH) * SCALE, ones(2 * SB_WIDTH),
        jnp.tile(q_norm[0], SWA_Q_HEADS) * SCALE, jnp.tile(k_norm[0], SWA_KV_HEADS), ones(SWA_KV_WIDTH)])[None]
    norm_flag = jnp.concatenate([jnp.zeros((3 * SB_WIDTH,), F32), ones(SWA_WIDTH + SWA_KV_WIDTH),
                                 jnp.zeros((SWA_KV_WIDTH,), F32)])[None]
    w_in_bf = w_in[0].astype(BF16)
    gain_mix = ln_mix[0][None]

    proj = _norm_proj(x2d, gain_mix, w_in_bf, col_scale, norm_flag)
    meta_proj = _norm_proj(meta_tokens.astype(F32), gain_mix, w_in_bf, col_scale, norm_flag)
    meta_blk = jnp.concatenate([jnp.zeros((META_PAD, IN_PROJ_WIDTH), BF16), meta_proj], axis=0)

    o_sb = _sb_attention(proj, meta_blk, batch, seq)
    o_sw = _swa_attention(proj, meta_blk, _bucket_tables(), rel_bias.astype(F32), sinks[0].astype(F32),
                          batch, seq)

    w_router = jnp.concatenate(
        [w_router_group[0], w_router_expert[0],
         jnp.zeros((D_MODEL, LANES - N_GROUPS - N_EXPERTS), F32)], axis=1).astype(BF16)
    b_router = jnp.concatenate(
        [b_router_group[0], b_router_expert[0], jnp.zeros((LANES - N_GROUPS - N_EXPERTS,), F32)])[None]
    h1, hfp, route = _out_router(o_sb, o_sw, x2d, out_norm_sb[0][None], out_norm_swa[0][None],
                                 w_out[0].astype(BF16), ln_ffn[0][None], w_router, b_router)

    dest, tok_sorted, tile_base, tile_expert, next_expert, w_slot, n_used = _routing_tables(route, n_tok)
    y = _experts(tile_expert, next_expert, w_slot, tile_base, n_used, tok_sorted, hfp,
                 w_gate[0], w_up[0], w_down[0])
    dest3 = jnp.concatenate([dest, jnp.zeros((2 * TOK_TILE,), I32)]).reshape(-1, 1, 2 * TOK_TILE)
    out = _combine(dest3, h1, route, y)
    return out.reshape(batch, seq, D_MODEL)
```

```python
import functools
import math

import numpy as np
import jax
import jax.numpy as jnp
from jax import lax
from jax.experimental import pallas as pl
from jax.experimental.pallas import tpu as pltpu

F32 = jnp.float32
BF16 = jnp.bfloat16
I32 = jnp.int32
U32 = jnp.uint32

D_MODEL = 2048
HEAD_DIM = 64
LANES = 128
SB_HEADS = 16
SWA_Q_HEADS = 16
SWA_KV_HEADS = 4
SB_WIDTH = SB_HEADS * HEAD_DIM
SWA_WIDTH = SWA_Q_HEADS * HEAD_DIM
SWA_KV_WIDTH = SWA_KV_HEADS * HEAD_DIM
IN_PROJ_WIDTH = 3 * SB_WIDTH + SWA_WIDTH + 2 * SWA_KV_WIDTH
BLOCK = 128
WINDOW = 128
N_META = 16
META_PAD = BLOCK - N_META
NUM_BUCKETS = 32
MAX_DISTANCE = 128
N_GROUPS = 8
EXPERTS_PER_GROUP = 8
N_EXPERTS = N_GROUPS * EXPERTS_PER_GROUP
D_EXPERT = D_MODEL // 4
RMS_EPS = 1e-6
SCALE = HEAD_DIM ** -0.5
LOG2E = 1.4426950408889634

COL_Q_SB = 0
COL_K_SB = SB_WIDTH // LANES
COL_V_SB = 2 * SB_WIDTH // LANES
COL_Q_SW = 3 * SB_WIDTH // LANES
COL_K_SW = COL_Q_SW + SWA_WIDTH // LANES
COL_V_SW = COL_K_SW + SWA_KV_WIDTH // LANES

PROJ_TN = 768
SB_CHUNK = 256
MOE_TILE = 256
TOK_TILE = 256
NEG_INF = float("-inf")


def _cparams(sem, vmem_mb):
    return pltpu.CompilerParams(dimension_semantics=sem, vmem_limit_bytes=vmem_mb * 1024 * 1024)


def _norm_proj_body(x_ref, g_ref, w_ref, cs_ref, nf_ref, o_ref, hn_ref, *, n_plain_tiles):
    j = pl.program_id(1)

    @pl.when(j == 0)
    def _():
        x = x_ref[...]
        var = jnp.mean(x * x, axis=-1, keepdims=True)
        hn_ref[...] = (x * lax.rsqrt(var + RMS_EPS) * g_ref[...]).astype(BF16)

    acc = jnp.dot(hn_ref[...], w_ref[...], preferred_element_type=F32)

    @pl.when(j < n_plain_tiles)
    def _():
        o_ref[...] = (acc * cs_ref[...]).astype(o_ref.dtype)

    @pl.when(j >= n_plain_tiles)
    def _():
        lane = lax.broadcasted_iota(I32, (1, LANES), 1)
        lo = lane < HEAD_DIM
        for c in range(acc.shape[1] // LANES):
            sl = slice(c * LANES, (c + 1) * LANES)
            a = acc[:, sl]
            s = a * a
            s_lo = jnp.sum(jnp.where(lo, s, 0.0), axis=-1, keepdims=True)
            s_hi = jnp.sum(jnp.where(lo, 0.0, s), axis=-1, keepdims=True)
            ms = jnp.where(lo, s_lo, s_hi) * (1.0 / HEAD_DIM)
            r = jnp.where(nf_ref[:, sl] > 0.0, lax.rsqrt(ms + RMS_EPS), 1.0)
            o_ref[:, sl] = ((a * r) * cs_ref[:, sl]).astype(o_ref.dtype)


def _norm_proj(x2d, gain, w_bf, col_scale, norm_flag):
    m = x2d.shape[0]
    tm = min(1024, m)
    n = w_bf.shape[1]
    n_tiles = n // PROJ_TN
    n_plain = (3 * SB_WIDTH) // PROJ_TN
    return pl.pallas_call(
        functools.partial(_norm_proj_body, n_plain_tiles=n_plain),
        grid=(m // tm, n_tiles),
        in_specs=[
            pl.BlockSpec((tm, D_MODEL), lambda i, j: (i, 0)),
            pl.BlockSpec((1, D_MODEL), lambda i, j: (0, 0)),
            pl.BlockSpec((D_MODEL, PROJ_TN), lambda i, j: (0, j)),
            pl.BlockSpec((1, PROJ_TN), lambda i, j: (0, j)),
            pl.BlockSpec((1, PROJ_TN), lambda i, j: (0, j)),
        ],
        out_specs=pl.BlockSpec((tm, PROJ_TN), lambda i, j: (i, j)),
        out_shape=jax.ShapeDtypeStruct((m, n), BF16),
        scratch_shapes=[pltpu.VMEM((tm, D_MODEL), BF16)],
        compiler_params=_cparams(("parallel", "arbitrary"), 48),
        name="norm_proj",
    )(x2d, gain, w_bf, col_scale, norm_flag)


SB_SKIP_LOG = 104.0


SB_QB = 4


def _sb_body(q_ref, k_ref, v_ref, km_ref, vm_ref, o_ref, kx_ref, vx_ref, acc_ref, car_ref, *, seq):
    g = pl.program_id(2)

    @pl.when(g == 0)
    def _():
        zeros = jnp.zeros((BLOCK, LANES), BF16)
        kx_ref[0:BLOCK, :] = zeros
        vx_ref[0:BLOCK, :] = zeros
        kx_ref[BLOCK:2 * BLOCK, :] = km_ref[...]
        vx_ref[BLOCK:2 * BLOCK, :] = vm_ref[...]
        kx_ref[2 * BLOCK:, :] = k_ref[...]
        vx_ref[2 * BLOCK:, :] = v_ref[...]

    lane = lax.broadcasted_iota(I32, (BLOCK, LANES), 1)
    lo = lane < HEAD_DIM

    def later_key_matrix(nk):
        r = lax.broadcasted_iota(I32, (nk, nk), 0)
        c = lax.broadcasted_iota(I32, (nk, nk), 1)
        return (r > c).astype(F32).astype(BF16)

    tri = {BLOCK: later_key_matrix(BLOCK), SB_CHUNK: later_key_matrix(SB_CHUNK)}

    def stage_qk(qm, start, nk):
        kc = kx_ref[pl.ds(start, nk), :]
        return lax.dot_general(qm, kc, (((1,), (1,)), ((), ())), preferred_element_type=F32)

    def stage_soft(s, start, mask):
        nk = s.shape[1]
        sp = jnp.maximum(s, 0.0) + jnp.log(1.0 + jnp.exp2(jnp.abs(s) * (-LOG2E)))
        log_beta = s - sp
        valid = None
        if mask is not None:
            r_i = lax.broadcasted_iota(I32, (2 * BLOCK, nk), 0)
            c_i = lax.broadcasted_iota(I32, (2 * BLOCK, nk), 1)
            if mask == "diag":
                valid = c_i < (r_i & (BLOCK - 1))
                sp = jnp.where(valid, sp, 0.0)
            else:
                sp = jnp.where((start + c_i) >= (BLOCK + META_PAD), sp, 0.0)
        return sp, log_beta, valid

    def stage_inner(soft, start):
        sp, log_beta, valid = soft
        nk = sp.shape[1]
        inner = jnp.dot(sp.astype(BF16), tri[nk], preferred_element_type=F32)
        return log_beta - inner, jnp.sum(sp, axis=1, keepdims=True), valid, vx_ref[pl.ds(start, nk), :]

    def scores(qm, start, nk, mask):
        return stage_inner(stage_soft(stage_qk(qm, start, nk), start, mask), start)

    def weights(part, car):
        log_w0, _, valid, _ = part
        nk = log_w0.shape[1]
        car_k = car if nk == LANES else jnp.concatenate([car, car], axis=1)
        w = jnp.exp(log_w0 - car_k)
        if valid is not None:
            w = jnp.where(valid, w, 0.0)
        return w.astype(BF16)

    def least_of(car):
        return jnp.min(jnp.min(car, axis=0, keepdims=True))

    idx, owns, qms = [], [], []
    for u in range(SB_QB):
        i = SB_QB * g + u
        idx.append(i)
        owns.append(pl.multiple_of(BLOCK * (i + 2), BLOCK))
        q = q_ref[u * BLOCK:(u + 1) * BLOCK, :]
        zq = jnp.zeros_like(q)
        qms.append(jnp.concatenate([jnp.where(lo, q, zq), jnp.where(lo, zq, q)], axis=0))
    chunks = []
    for u in range(SB_QB):
        chunks.append((u, owns[u], BLOCK, "diag"))
        chunks.append((u, pl.multiple_of(owns[u] - SB_CHUNK, BLOCK), SB_CHUNK, "low"))
    qk = [stage_qk(qms[u], start, nk) for u, start, nk, _ in chunks]
    soft = [stage_soft(s, start, mask) for s, (_, start, _, mask) in zip(qk, chunks)]
    parts = [stage_inner(sf, start) for sf, (_, start, _, _) in zip(soft, chunks)]
    tiles = []
    for u in range(SB_QB):
        diag, low1 = parts[2 * u], parts[2 * u + 1]
        car = jnp.broadcast_to(diag[1], (2 * BLOCK, LANES))
        acc = jnp.dot(jnp.where(diag[2], jnp.exp(diag[0]), 0.0).astype(BF16), diag[3],
                      preferred_element_type=F32)
        acc = acc + jnp.dot(weights(low1, car), low1[3], preferred_element_type=F32)
        car = car + low1[1]
        acc_ref[u] = acc
        car_ref[u] = car
        tiles.append((idx[u], owns[u], qms[u], least_of(car)))

    for u, (i, own, qm, least0) in enumerate(tiles):
        n_low = lax.shift_right_logical(i + 2, 1)

        def step(start, mask, u=u, qm=qm):
            part = scores(qm, start, SB_CHUNK, mask)
            car_ = car_ref[u]
            acc_ref[u] += jnp.dot(weights(part, car_), part[3], preferred_element_type=F32)
            new_car = car_ + part[1]
            car_ref[u] = new_car
            return least_of(new_car)

        def cond(st, n_low=n_low):
            j, least_ = st
            return (j < n_low) & (least_ <= SB_SKIP_LOG)

        def body(st, own=own, step=step):
            j, _ = st
            return j + 1, step(pl.multiple_of(own - j * SB_CHUNK, BLOCK), None)

        j_end, least = lax.while_loop(cond, body, (jnp.int32(2), least0))

        @pl.when((n_low >= 2) & (j_end == n_low) & (least <= SB_SKIP_LOG))
        def _(own=own, n_low=n_low, step=step):
            step(pl.multiple_of(own - n_low * SB_CHUNK, BLOCK), "low")

        acc = acc_ref[u]
        o_ref[u * BLOCK:(u + 1) * BLOCK, :] = jnp.where(lo, acc[0:BLOCK], acc[BLOCK:2 * BLOCK])


def _sb_attention(proj, meta_blk, batch, seq):
    nq = seq // BLOCK
    ng = nq // SB_QB
    n_pairs = SB_WIDTH // LANES
    rows = SB_QB * BLOCK
    return pl.pallas_call(
        functools.partial(_sb_body, seq=seq),
        grid=(batch, n_pairs, ng),
        in_specs=[
            pl.BlockSpec((rows, LANES), lambda b, h, g: (b * ng + g, COL_Q_SB + h)),
            pl.BlockSpec((seq, LANES), lambda b, h, g: (b, COL_K_SB + h)),
            pl.BlockSpec((seq, LANES), lambda b, h, g: (b, COL_V_SB + h)),
            pl.BlockSpec((BLOCK, LANES), lambda b, h, g: (0, COL_K_SB + h)),
            pl.BlockSpec((BLOCK, LANES), lambda b, h, g: (0, COL_V_SB + h)),
        ],
        out_specs=pl.BlockSpec((rows, LANES), lambda b, h, g: (b * ng + g, h)),
        out_shape=jax.ShapeDtypeStruct((batch * seq, SB_WIDTH), F32),
        scratch_shapes=[
            pltpu.VMEM((seq + 2 * BLOCK, LANES), BF16),
            pltpu.VMEM((seq + 2 * BLOCK, LANES), BF16),
            pltpu.VMEM((SB_QB, 2 * BLOCK, LANES), F32),
            pltpu.VMEM((SB_QB, 2 * BLOCK, LANES), F32),
        ],
        compiler_params=_cparams(("parallel", "parallel", "arbitrary"), 32),
        name="sb_attn",
    )(proj, proj, proj, meta_blk, meta_blk)


SWA_QB = 4


def _swa_body(q_ref, kp_ref, kc_ref, vp_ref, vc_ref, km_ref, vm_ref, bkt_ref,
              rel_ref, sink_ref, o_ref, bias_ref):
    t = pl.program_id(1)
    g = pl.program_id(2)
    h0 = 2 * t
    kv_in_hi = (lax.shift_right_logical(t, 1) & 1) == 1

    @pl.when(g == 0)
    def _():
        for tb in range(2):
            bkt = bkt_ref[tb]
            for hh in range(2):
                bias = jnp.full(bkt.shape, NEG_INF, F32)
                for b in range(NUM_BUCKETS):
                    bias = jnp.where(bkt == b, rel_ref[b, h0 + hh], bias)
                bias_ref[tb, hh * BLOCK:(hh + 1) * BLOCK, :] = bias

    lane = lax.broadcasted_iota(I32, (BLOCK, LANES), 1)
    lo = lane < HEAD_DIM
    kmask = jnp.logical_xor(lo, kv_in_hi)
    rcol = lax.broadcasted_iota(I32, (2 * BLOCK, 1), 0)
    sink = jnp.where(rcol < BLOCK, sink_ref[h0], sink_ref[h0 + 1])
    k_meta = km_ref[...]
    v_meta = vm_ref[...]

    blocks = range(SWA_QB)
    rows = [slice(u * BLOCK, (u + 1) * BLOCK) for u in blocks]
    scores, values = [], []
    for u in blocks:
        q = q_ref[rows[u], :]
        q_sw = jnp.concatenate([q[:, HEAD_DIM:], q[:, :HEAD_DIM]], axis=1)
        zq = jnp.zeros_like(q)
        qa = jnp.where(kmask, jnp.where(kv_in_hi, q_sw, q), zq)
        qb = jnp.where(kmask, jnp.where(kv_in_hi, q, q_sw), zq)
        qm = jnp.concatenate([qa, qb], axis=0)
        k_prev = kp_ref[...] if u == 0 else kc_ref[rows[u - 1], :]
        v_prev = vp_ref[...] if u == 0 else vc_ref[rows[u - 1], :]
        k_all = jnp.concatenate([k_prev, kc_ref[rows[u], :], k_meta], axis=0)
        values.append(jnp.concatenate([v_prev, vc_ref[rows[u], :], v_meta], axis=0))
        bias = bias_ref[jnp.minimum(g, 1)] if u == 0 else bias_ref[1]
        scores.append(lax.dot_general(qm, k_all, (((1,), (1,)), ((), ())), preferred_element_type=F32) + bias)
    peaks = [jnp.maximum(jnp.max(s, axis=-1, keepdims=True), sink) for s in scores]
    probs = [jnp.exp(s - m) for s, m in zip(scores, peaks)]
    denoms = [jnp.sum(p, axis=-1, keepdims=True) + jnp.exp(sink - m) for p, m in zip(probs, peaks)]
    outs = [jnp.dot((p * (1.0 / d)).astype(BF16), v, preferred_element_type=F32)
            for p, d, v in zip(probs, denoms, values)]
    for u in blocks:
        oa = outs[u][0:BLOCK]
        ob = outs[u][BLOCK:2 * BLOCK]
        oa_r = pltpu.roll(oa, HEAD_DIM, 1)
        ob_r = pltpu.roll(ob, HEAD_DIM, 1)
        o_ref[rows[u], :] = jnp.where(lo, jnp.where(kv_in_hi, oa_r, oa), jnp.where(kv_in_hi, ob, ob_r))


def _swa_attention(proj, meta_blk, bkt, rel_bias, sinks, batch, seq):
    nq = seq // BLOCK
    ng = nq // SWA_QB
    n_pairs = SWA_WIDTH // LANES
    rows = SWA_QB * BLOCK

    def cur(base):
        return lambda b, t, g: (b * ng + g, base + t // 4)

    def prev(base):
        return lambda b, t, g: (b * nq + jnp.maximum(SWA_QB * g - 1, 0), base + t // 4)

    smem = pl.BlockSpec(memory_space=pltpu.SMEM)
    return pl.pallas_call(
        _swa_body,
        grid=(batch, n_pairs, ng),
        in_specs=[
            pl.BlockSpec((rows, LANES), lambda b, t, g: (b * ng + g, COL_Q_SW + t)),
            pl.BlockSpec((BLOCK, LANES), prev(COL_K_SW)),
            pl.BlockSpec((rows, LANES), cur(COL_K_SW)),
            pl.BlockSpec((BLOCK, LANES), prev(COL_V_SW)),
            pl.BlockSpec((rows, LANES), cur(COL_V_SW)),
            pl.BlockSpec((BLOCK, LANES), lambda b, t, g: (0, COL_K_SW + t // 4)),
            pl.BlockSpec((BLOCK, LANES), lambda b, t, g: (0, COL_V_SW + t // 4)),
            pl.BlockSpec((2, BLOCK, 3 * BLOCK), lambda b, t, g: (0, 0, 0)),
            smem,
            smem,
        ],
        out_specs=pl.BlockSpec((rows, LANES), lambda b, t, g: (b * ng + g, t)),
        out_shape=jax.ShapeDtypeStruct((batch * seq, SWA_WIDTH), F32),
        scratch_shapes=[pltpu.VMEM((2, 2 * BLOCK, 3 * BLOCK), F32)],
        compiler_params=_cparams(("parallel", "parallel", "arbitrary"), 32),
        name="swa_attn",
    )(proj, proj, proj, proj, proj, meta_blk, meta_blk, bkt, rel_bias, sinks)


def _t5_bucket_np(dist):
    max_exact = NUM_BUCKETS // 2
    d = np.maximum(dist, 0)
    ratio = np.maximum(d, 1).astype(np.float32) / np.float32(max_exact)
    large = max_exact + (np.log(ratio) / np.float32(math.log(MAX_DISTANCE / max_exact))
                         * np.float32(NUM_BUCKETS - max_exact)).astype(np.int32)
    large = np.minimum(large, NUM_BUCKETS - 1)
    return np.where(d < max_exact, d, large).astype(np.int32)


def _bucket_tables():
    ql = np.arange(BLOCK)[:, None]
    sl = np.arange(2 * BLOCK)[None, :]
    dist = ql + BLOCK - sl
    ok = (dist >= 0) & (dist < WINDOW)
    band = np.where(ok, _t5_bucket_np(dist), -1).astype(np.int32)
    band_first = np.where(sl >= BLOCK, band, -1).astype(np.int32)
    ms = np.arange(BLOCK)[None, :]
    tables = []
    for blk, b in ((0, band_first), (1, band)):
        dist_m = ql + (blk + 1) * BLOCK - ms
        meta = np.where(ms >= META_PAD, _t5_bucket_np(dist_m), -1).astype(np.int32)
        tables.append(np.concatenate([b, meta], axis=1))
    return jnp.asarray(np.stack(tables))


def _out_router_body(osb_ref, osw_ref, x_ref, gsb_ref, gsw_ref, wo_ref, lf_ref, wr_ref, br_ref,
                     h1_ref, hfp_ref, route_ref):
    def branch_norm(o_ref_, g_ref_):
        a = o_ref_[...]
        var = jnp.mean(a * a, axis=-1, keepdims=True)
        return ((a * lax.rsqrt(var + RMS_EPS)) * g_ref_[...]).astype(BF16)

    mixed = jnp.concatenate([branch_norm(osb_ref, gsb_ref), branch_norm(osw_ref, gsw_ref)], axis=1)
    h1 = x_ref[...] + jnp.dot(mixed, wo_ref[...], preferred_element_type=F32)
    h1_ref[...] = h1
    var = jnp.mean(h1 * h1, axis=-1, keepdims=True)
    hf = ((h1 * lax.rsqrt(var + RMS_EPS)) * lf_ref[...]).astype(BF16)

    half = D_MODEL // 2
    lo_bits = pltpu.bitcast(hf[:, :half].astype(F32), U32)
    hi_bits = pltpu.bitcast(hf[:, half:].astype(F32), U32)
    hfp_ref[...] = (hi_bits & jnp.uint32(0xFFFF0000)) | (lo_bits >> 16)

    logits = jnp.dot(hf, wr_ref[...], preferred_element_type=F32) + br_ref[...]
    tm = logits.shape[0]
    lane = lax.broadcasted_iota(I32, (tm, LANES), 1)
    big = jnp.int32(LANES)

    def softmax_masked(mask):
        z = jnp.where(mask, logits, NEG_INF)
        zmax = jnp.max(z, axis=-1, keepdims=True)
        e = jnp.exp(z - zmax)
        return e / jnp.sum(e, axis=-1, keepdims=True)

    def top1(p, mask):
        pm = jnp.where(mask, p, -1.0)
        v = jnp.max(pm, axis=-1, keepdims=True)
        idx = jnp.min(jnp.where(pm == v, lane, big), axis=-1, keepdims=True)
        return v, idx

    gmask = lane < N_GROUPS
    g_val, g_idx = top1(softmax_masked(gmask), gmask)
    e_lo = N_GROUPS + EXPERTS_PER_GROUP * g_idx
    emask = (lane >= e_lo) & (lane < e_lo + EXPERTS_PER_GROUP)
    e_prob = softmax_masked(emask)
    v1, i1 = top1(e_prob, emask)
    emask2 = emask & (lane != i1)
    v2, i2 = top1(e_prob, emask2)
    norm = v1 + v2
    gate1 = g_val * v1 / norm
    gate2 = g_val * v2 / norm
    ex1 = (i1 - N_GROUPS).astype(F32)
    ex2 = (i2 - N_GROUPS).astype(F32)
    route_ref[...] = jnp.where(lane == 0, gate1,
                               jnp.where(lane == 1, gate2,
                                         jnp.where(lane == 2, ex1,
                                                   jnp.where(lane == 3, ex2, 0.0))))


def _out_router(o_sb, o_sw, x2d, g_sb, g_sw, wo_bf, ln_ffn, w_router, b_router):
    m = x2d.shape[0]
    tm = TOK_TILE
    row = lambda w: pl.BlockSpec((tm, w), lambda i: (i, 0))
    const = lambda r, c: pl.BlockSpec((r, c), lambda i: (0, 0))
    return pl.pallas_call(
        _out_router_body,
        grid=(m // tm,),
        in_specs=[row(SB_WIDTH), row(SWA_WIDTH), row(D_MODEL), const(1, SB_WIDTH), const(1, SWA_WIDTH),
                  const(D_MODEL, D_MODEL), const(1, D_MODEL), const(D_MODEL, LANES), const(1, LANES)],
        out_specs=[row(D_MODEL), row(D_MODEL // 2), row(LANES)],
        out_shape=[jax.ShapeDtypeStruct((m, D_MODEL), F32),
                   jax.ShapeDtypeStruct((m, D_MODEL // 2), U32),
                   jax.ShapeDtypeStruct((m, LANES), F32)],
        compiler_params=_cparams(("parallel",), 56),
        name="out_router",
    )(o_sb, o_sw, x2d, g_sb, g_sw, wo_bf, ln_ffn, w_router, b_router)


W_DMA_PRIORITY = 1


def _experts_body(te_ref, nx_ref, ws_ref, base_ref, nu_ref, tok_ref, hfp_ref, wg_ref, wu_ref, wd_ref, y_ref,
                  xbuf, wg_f, wu_f, wd_f, wg_bf, wu_bf, wd_bf, gsem, wsem):
    n = pl.program_id(0)
    n_used = nu_ref[0]

    def row_copy(tile, r, slot):
        tok = tok_ref[base_ref[tile] + r]
        return pltpu.make_async_copy(hfp_ref.at[pl.ds(tok, 1), :], xbuf.at[slot, pl.ds(r, 1), :], gsem.at[slot])

    def row_wait(r, slot):
        pltpu.make_async_copy(hfp_ref.at[pl.ds(0, 1), :], xbuf.at[slot, pl.ds(r, 1), :], gsem.at[slot]).wait()

    def weight_copies(e, slot):
        return (pltpu.make_async_copy(wg_ref.at[e], wg_f.at[slot], wsem.at[slot, 0]),
                pltpu.make_async_copy(wu_ref.at[e], wu_f.at[slot], wsem.at[slot, 1]),
                pltpu.make_async_copy(wd_ref.at[e], wd_f.at[slot], wsem.at[slot, 2]))

    @pl.when(n == 0)
    def _():
        for c in weight_copies(te_ref[0], 0):
            c.start(priority=W_DMA_PRIORITY)
        for r in range(MOE_TILE):
            row_copy(0, r, 0).start()

    @pl.when(n < n_used)
    def _():
        prev = te_ref[jnp.maximum(n - 1, 0)]
        fresh = (n == 0) | (te_ref[n] != prev)

        @pl.when(fresh)
        def _():
            ws = ws_ref[n]
            for c in weight_copies(te_ref[n], ws):
                c.wait()

            @pl.when(nx_ref[n] >= 0)
            def _():
                for c in weight_copies(nx_ref[n], 1 - ws):
                    c.start(priority=W_DMA_PRIORITY)

            wg_bf[...] = wg_f[ws].astype(BF16)
            wu_bf[...] = wu_f[ws].astype(BF16)
            wd_bf[...] = wd_f[ws].astype(BF16)

        slot = n & 1
        for r in range(MOE_TILE):
            row_copy(n + 1, r, 1 - slot).start()
        for r in range(MOE_TILE):
            row_wait(r, slot)

        u = xbuf[slot]
        x_lo = pltpu.bitcast(u << 16, F32).astype(BF16)
        x_hi = pltpu.bitcast(u & jnp.uint32(0xFFFF0000), F32).astype(BF16)
        x = jnp.concatenate([x_lo, x_hi], axis=1)
        g = jnp.dot(x, wg_bf[...], preferred_element_type=F32)
        up = jnp.dot(x, wu_bf[...], preferred_element_type=F32)
        hdn = (g * jax.nn.sigmoid(g) * up).astype(BF16)
        y_ref[...] = jnp.dot(hdn, wd_bf[...], preferred_element_type=F32)

    @pl.when(n == n_used)
    def _():
        for r in range(MOE_TILE):
            row_wait(r, n & 1)

    @pl.when(n >= n_used)
    def _():
        y_ref[...] = jnp.zeros_like(y_ref)


def _experts(tile_expert, next_expert, w_slot, tile_base, n_used, tok_sorted, hfp, w_gate, w_up, w_down):
    n_tiles = tile_expert.shape[0]
    any_space = pl.BlockSpec(memory_space=pl.ANY)
    grid_spec = pltpu.PrefetchScalarGridSpec(
        num_scalar_prefetch=6,
        grid=(n_tiles + 1,),
        in_specs=[any_space, any_space, any_space, any_space],
        out_specs=pl.BlockSpec((MOE_TILE, D_MODEL), lambda n, *_: (n, 0)),
        scratch_shapes=[pltpu.VMEM((2, MOE_TILE, D_MODEL // 2), U32),
                        pltpu.VMEM((2, D_MODEL, D_EXPERT), F32),
                        pltpu.VMEM((2, D_MODEL, D_EXPERT), F32),
                        pltpu.VMEM((2, D_EXPERT, D_MODEL), F32),
                        pltpu.VMEM((D_MODEL, D_EXPERT), BF16),
                        pltpu.VMEM((D_MODEL, D_EXPERT), BF16),
                        pltpu.VMEM((D_EXPERT, D_MODEL), BF16),
                        pltpu.SemaphoreType.DMA((2,)),
                        pltpu.SemaphoreType.DMA((2, 3))],
    )
    return pl.pallas_call(
        _experts_body,
        grid_spec=grid_spec,
        out_shape=jax.ShapeDtypeStruct(((n_tiles + 1) * MOE_TILE, D_MODEL), F32),
        compiler_params=_cparams(("arbitrary",), 56),
        name="moe_experts",
    )(tile_expert, next_expert, w_slot, tile_base, n_used, tok_sorted, hfp, w_gate, w_up, w_down)


def _combine_body(dcur_ref, dnxt_ref, h1_ref, route_ref, y_ref, o_ref, ybuf, sem, *, n_steps):
    i = pl.program_id(0)
    n = dcur_ref.shape[2]

    def row_copy(idx_ref, a, slot):
        return pltpu.make_async_copy(y_ref.at[pl.ds(idx_ref[0, 0, a], 1), :],
                                     ybuf.at[slot, a % 2, pl.ds(a // 2, 1), :], sem.at[slot])

    @pl.when(i == 0)
    def _():
        for a in range(n):
            row_copy(dcur_ref, a, 0).start()

    @pl.when(i < n_steps)
    def _():
        slot = i & 1
        for a in range(n):
            row_copy(dnxt_ref, a, 1 - slot).start(priority=a % 2)
        for a in range(n):
            row_copy(dcur_ref, a, slot).wait()
        route = route_ref[...]
        g0 = route[:, 0:1]
        g1 = route[:, 1:2]
        o_ref[...] = h1_ref[...] + (ybuf[slot, 0] * g0 + ybuf[slot, 1] * g1)

    @pl.when(i == n_steps)
    def _():
        for a in range(n):
            row_copy(dcur_ref, a, i & 1).wait()


def _combine(dest3, h1, route, y):
    m = h1.shape[0]
    tm = dest3.shape[2] // 2
    n_steps = m // tm
    tok = lambda i: (jnp.minimum(i, n_steps - 1), 0)
    return pl.pallas_call(
        functools.partial(_combine_body, n_steps=n_steps),
        grid=(n_steps + 1,),
        in_specs=[
            pl.BlockSpec((1, 1, 2 * tm), lambda i: (i, 0, 0), memory_space=pltpu.SMEM),
            pl.BlockSpec((1, 1, 2 * tm), lambda i: (jnp.minimum(i + 1, n_steps), 0, 0), memory_space=pltpu.SMEM),
            pl.BlockSpec((tm, D_MODEL), tok),
            pl.BlockSpec((tm, LANES), tok),
            pl.BlockSpec(memory_space=pl.ANY),
        ],
        out_specs=pl.BlockSpec((tm, D_MODEL), tok),
        out_shape=jax.ShapeDtypeStruct((m, D_MODEL), F32),
        scratch_shapes=[pltpu.VMEM((2, 2, tm, D_MODEL), F32), pltpu.SemaphoreType.DMA((2,))],
        compiler_params=_cparams(("arbitrary",), 40),
        name="moe_combine",
    )(dest3, dest3, h1, route, y)


def _routing_tables(route, n_tok):
    a_expert = route[:, 2:4].astype(I32).reshape(-1)
    n_assign = a_expert.shape[0]
    eids = jnp.arange(N_EXPERTS, dtype=I32)
    sorted_e, order = lax.sort((a_expert, jnp.arange(n_assign, dtype=I32)), num_keys=1, is_stable=True)
    first = jnp.searchsorted(sorted_e, eids, side="left").astype(I32)
    counts = jnp.concatenate([first[1:], jnp.full((1,), n_assign, I32)]) - first
    padded = (counts + MOE_TILE - 1) // MOE_TILE * MOE_TILE
    seg_end = jnp.cumsum(padded)
    seg_start = seg_end - padded
    shift = seg_start - first
    shift_p = jnp.sum(jnp.where(sorted_e[None, :] == eids[:, None], shift[:, None], 0), axis=0)
    _, dest = lax.sort((order, jnp.arange(n_assign, dtype=I32) + shift_p), num_keys=1)

    n_tiles = n_assign // MOE_TILE + N_EXPERTS
    n_used = (seg_end[-1] // MOE_TILE).astype(I32)
    tile_start = jnp.arange(n_tiles + 2, dtype=I32) * MOE_TILE
    te = jnp.minimum(jnp.searchsorted(seg_end, tile_start, side="right"), N_EXPERTS - 1).astype(I32)
    used = counts > 0
    later_used = used[None, :] & (eids[None, :] > eids[:, None])
    next_e = jnp.min(jnp.where(later_used, eids[None, :], N_EXPERTS), axis=1)
    next_e = jnp.where(next_e == N_EXPERTS, -1, next_e).astype(I32)
    ordinal = jnp.cumsum(used.astype(I32)) - 1
    base = first[te] + tile_start - seg_start[te]
    base = jnp.where(jnp.arange(n_tiles + 2) < n_used, jnp.clip(base, 0, n_assign), n_assign).astype(I32)
    tok_sorted = jnp.concatenate([order // 2, jnp.zeros((MOE_TILE,), I32)])
    return (dest.astype(I32), tok_sorted, base, te[:n_tiles], next_e[te[:n_tiles]],
            (ordinal[te[:n_tiles]] & 1).astype(I32), n_used.reshape(1))


def kernel(x, meta_tokens, rel_bias, ln_mix, w_in, q_norm, k_norm, sinks, out_norm_sb, out_norm_swa,
           w_out, ln_ffn, w_router_group, b_router_group, w_router_expert, b_router_expert,
           w_gate, w_up, w_down):
    batch, seq, _ = x.shape
    n_tok = batch * seq
    x2d = x.reshape(n_tok, D_MODEL)

    ones = lambda n: jnp.ones((n,), F32)
    col_scale = jnp.concatenate([
        ones(SB_WIDTH) * SCALE, ones(2 * SB_WIDTH),
        jnp.tile(q_norm[0], SWA_Q_HEADS) * SCALE, jnp.tile(k_norm[0], SWA_KV_HEADS), ones(SWA_KV_WIDTH)])[None]
    norm_flag = jnp.concatenate([jnp.zeros((3 * SB_WIDTH,), F32), ones(SWA_WIDTH + SWA_KV_WIDTH),
                                 jnp.zeros((SWA_KV_WIDTH,), F32)])[None]
    w_in_bf = w_in[0].astype(BF16)
    gain_mix = ln_mix[0][None]

    proj = _norm_proj(x2d, gain_mix, w_in_bf, col_scale, norm_flag)
    meta_proj = _norm_proj(meta_tokens.astype(F32), gain_mix, w_in_bf, col_scale, norm_flag)
    meta_blk = jnp.concatenate([jnp.zeros((META_PAD, IN_PROJ_WIDTH), BF16), meta_proj], axis=0)

    o_sb = _sb_attention(proj, meta_blk, batch, seq)
    o_sw = _swa_attention(proj, meta_blk, _bucket_tables(), rel_bias.astype(F32), sinks[0].astype(F32),
                          batch, seq)

    w_router = jnp.concatenate(
        [w_router_group[0], w_router_expert[0],
         jnp.zeros((D_MODEL, LANES - N_GROUPS - N_EXPERTS), F32)], axis=1).astype(BF16)
    b_router = jnp.concatenate(
        [b_router_group[0], b_router_expert[0], jnp.zeros((LANES - N_GROUPS - N_EXPERTS,), F32)])[None]
    h1, hfp, route = _out_router(o_sb, o_sw, x2d, out_norm_sb[0][None], out_norm_swa[0][None],
                                 w_out[0].astype(BF16), ln_ffn[0][None], w_router, b_router)

    dest, tok_sorted, tile_base, tile_expert, next_expert, w_slot, n_used = _routing_tables(route, n_tok)
    y = _experts(tile_expert, next_expert, w_slot, tile_base, n_used, tok_sorted, hfp,
                 w_gate[0], w_up[0], w_down[0])
    dest3 = jnp.concatenate([dest, jnp.zeros((2 * TOK_TILE,), I32)]).reshape(-1, 1, 2 * TOK_TILE)
    out = _combine(dest3, h1, route, y)
    return out.reshape(batch, seq, D_MODEL)
```

```python
import functools
import math

import numpy as np
import jax
import jax.numpy as jnp
from jax import lax
from jax.experimental import pallas as pl
from jax.experimental.pallas import tpu as pltpu

F32 = jnp.float32
BF16 = jnp.bfloat16
I32 = jnp.int32
U32 = jnp.uint32

D_MODEL = 2048
HEAD_DIM = 64
LANES = 128
SB_HEADS = 16
SWA_Q_HEADS = 16
SWA_KV_HEADS = 4
SB_WIDTH = SB_HEADS * HEAD_DIM
SWA_WIDTH = SWA_Q_HEADS * HEAD_DIM
SWA_KV_WIDTH = SWA_KV_HEADS * HEAD_DIM
IN_PROJ_WIDTH = 3 * SB_WIDTH + SWA_WIDTH + 2 * SWA_KV_WIDTH
BLOCK = 128
WINDOW = 128
N_META = 16
META_PAD = BLOCK - N_META
NUM_BUCKETS = 32
MAX_DISTANCE = 128
N_GROUPS = 8
EXPERTS_PER_GROUP = 8
N_EXPERTS = N_GROUPS * EXPERTS_PER_GROUP
D_EXPERT = D_MODEL // 4
RMS_EPS = 1e-6
SCALE = HEAD_DIM ** -0.5
LOG2E = 1.4426950408889634

COL_Q_SB = 0
COL_K_SB = SB_WIDTH // LANES
COL_V_SB = 2 * SB_WIDTH // LANES
COL_Q_SW = 3 * SB_WIDTH // LANES
COL_K_SW = COL_Q_SW + SWA_WIDTH // LANES
COL_V_SW = COL_K_SW + SWA_KV_WIDTH // LANES

PROJ_TN = 768
PROJ_TM = 512
OUT_TM = 512
SB_CHUNK = 256
MOE_TILE = 256
TOK_TILE = 256
NEG_INF = float("-inf")


def _cparams(sem, vmem_mb):
    return pltpu.CompilerParams(dimension_semantics=sem, vmem_limit_bytes=vmem_mb * 1024 * 1024)


def _norm_proj_body(x_ref, g_ref, w_ref, cs_ref, nf_ref, o_ref, *, n_plain_tiles):
    x = x_ref[...]
    var = jnp.mean(x * x, axis=-1, keepdims=True)
    hn = (x * lax.rsqrt(var + RMS_EPS) * g_ref[...]).astype(BF16)
    lane = lax.broadcasted_iota(I32, (1, LANES), 1)
    lo = lane < HEAD_DIM
    for j in range(w_ref.shape[1] // PROJ_TN):
        cols = slice(j * PROJ_TN, (j + 1) * PROJ_TN)
        acc = jnp.dot(hn, w_ref[:, cols], preferred_element_type=F32)
        if j < n_plain_tiles:
            o_ref[:, cols] = (acc * cs_ref[:, cols]).astype(o_ref.dtype)
            continue
        for c in range(PROJ_TN // LANES):
            sl = slice(j * PROJ_TN + c * LANES, j * PROJ_TN + (c + 1) * LANES)
            a = acc[:, c * LANES:(c + 1) * LANES]
            s = a * a
            s_lo = jnp.sum(jnp.where(lo, s, 0.0), axis=-1, keepdims=True)
            s_hi = jnp.sum(jnp.where(lo, 0.0, s), axis=-1, keepdims=True)
            ms = jnp.where(lo, s_lo, s_hi) * (1.0 / HEAD_DIM)
            r = jnp.where(nf_ref[:, sl] > 0.0, lax.rsqrt(ms + RMS_EPS), 1.0)
            o_ref[:, sl] = ((a * r) * cs_ref[:, sl]).astype(o_ref.dtype)


def _norm_proj(x2d, gain, w_bf, col_scale, norm_flag):
    m = x2d.shape[0]
    tm = min(PROJ_TM, m)
    n = w_bf.shape[1]
    n_plain = (3 * SB_WIDTH) // PROJ_TN
    whole = lambda r, c: pl.BlockSpec((r, c), lambda i: (0, 0))
    return pl.pallas_call(
        functools.partial(_norm_proj_body, n_plain_tiles=n_plain),
        grid=(m // tm,),
        in_specs=[pl.BlockSpec((tm, D_MODEL), lambda i: (i, 0)), whole(1, D_MODEL), whole(D_MODEL, n),
                  whole(1, n), whole(1, n)],
        out_specs=pl.BlockSpec((tm, n), lambda i: (i, 0)),
        out_shape=jax.ShapeDtypeStruct((m, n), BF16),
        compiler_params=_cparams(("parallel",), 56),
        name="norm_proj",
    )(x2d, gain, w_bf, col_scale, norm_flag)


SB_SKIP_LOG = 104.0


SB_QB = 4


def _sb_body(q_ref, k_ref, v_ref, km_ref, vm_ref, o_ref, kx_ref, vx_ref, acc_ref, car_ref, *, seq):
    g = pl.program_id(2)

    @pl.when(g == 0)
    def _():
        zeros = jnp.zeros((BLOCK, LANES), BF16)
        kx_ref[0:BLOCK, :] = zeros
        vx_ref[0:BLOCK, :] = zeros
        kx_ref[BLOCK:2 * BLOCK, :] = km_ref[...]
        vx_ref[BLOCK:2 * BLOCK, :] = vm_ref[...]
        kx_ref[2 * BLOCK:, :] = k_ref[...]
        vx_ref[2 * BLOCK:, :] = v_ref[...]

    lane = lax.broadcasted_iota(I32, (BLOCK, LANES), 1)
    lo = lane < HEAD_DIM

    def later_key_matrix(nk):
        r = lax.broadcasted_iota(I32, (nk, nk), 0)
        c = lax.broadcasted_iota(I32, (nk, nk), 1)
        return (r > c).astype(F32).astype(BF16)

    tri = {BLOCK: later_key_matrix(BLOCK), SB_CHUNK: later_key_matrix(SB_CHUNK)}

    def stage_qk(qm, start, nk):
        kc = kx_ref[pl.ds(start, nk), :]
        return lax.dot_general(qm, kc, (((1,), (1,)), ((), ())), preferred_element_type=F32)

    def stage_soft(s, start, mask):
        nk = s.shape[1]
        sp = jnp.maximum(s, 0.0) + jnp.log(1.0 + jnp.exp2(jnp.abs(s) * (-LOG2E)))
        log_beta = s - sp
        valid = None
        if mask is not None:
            r_i = lax.broadcasted_iota(I32, (2 * BLOCK, nk), 0)
            c_i = lax.broadcasted_iota(I32, (2 * BLOCK, nk), 1)
            if mask == "diag":
                valid = c_i < (r_i & (BLOCK - 1))
                sp = jnp.where(valid, sp, 0.0)
            else:
                sp = jnp.where((start + c_i) >= (BLOCK + META_PAD), sp, 0.0)
        return sp, log_beta, valid

    def stage_inner(soft, start):
        sp, log_beta, valid = soft
        nk = sp.shape[1]
        inner = jnp.dot(sp.astype(BF16), tri[nk], preferred_element_type=F32)
        return log_beta - inner, jnp.sum(sp, axis=1, keepdims=True), valid, vx_ref[pl.ds(start, nk), :]

    def scores(qm, start, nk, mask):
        return stage_inner(stage_soft(stage_qk(qm, start, nk), start, mask), start)

    def weights(part, car):
        log_w0, _, valid, _ = part
        nk = log_w0.shape[1]
        car_k = car if nk == LANES else jnp.concatenate([car, car], axis=1)
        w = jnp.exp(log_w0 - car_k)
        if valid is not None:
            w = jnp.where(valid, w, 0.0)
        return w.astype(BF16)

    def least_of(car):
        return jnp.min(jnp.min(car, axis=0, keepdims=True))

    idx, owns, qms = [], [], []
    for u in range(SB_QB):
        i = SB_QB * g + u
        idx.append(i)
        owns.append(pl.multiple_of(BLOCK * (i + 2), BLOCK))
        q = q_ref[u * BLOCK:(u + 1) * BLOCK, :]
        zq = jnp.zeros_like(q)
        qms.append(jnp.concatenate([jnp.where(lo, q, zq), jnp.where(lo, zq, q)], axis=0))
    chunks = []
    for u in range(SB_QB):
        chunks.append((u, owns[u], BLOCK, "diag"))
        chunks.append((u, pl.multiple_of(owns[u] - SB_CHUNK, BLOCK), SB_CHUNK, "low"))
    qk = [stage_qk(qms[u], start, nk) for u, start, nk, _ in chunks]
    soft = [stage_soft(s, start, mask) for s, (_, start, _, mask) in zip(qk, chunks)]
    parts = [stage_inner(sf, start) for sf, (_, start, _, _) in zip(soft, chunks)]
    tiles = []
    for u in range(SB_QB):
        diag, low1 = parts[2 * u], parts[2 * u + 1]
        car = jnp.broadcast_to(diag[1], (2 * BLOCK, LANES))
        acc = jnp.dot(jnp.where(diag[2], jnp.exp(diag[0]), 0.0).astype(BF16), diag[3],
                      preferred_element_type=F32)
        acc = acc + jnp.dot(weights(low1, car), low1[3], preferred_element_type=F32)
        car = car + low1[1]
        acc_ref[u] = acc
        car_ref[u] = car
        tiles.append((idx[u], owns[u], qms[u], least_of(car)))

    for u, (i, own, qm, least0) in enumerate(tiles):
        n_low = lax.shift_right_logical(i + 2, 1)

        def step(start, mask, u=u, qm=qm):
            part = scores(qm, start, SB_CHUNK, mask)
            car_ = car_ref[u]
            acc_ref[u] += jnp.dot(weights(part, car_), part[3], preferred_element_type=F32)
            new_car = car_ + part[1]
            car_ref[u] = new_car
            return least_of(new_car)

        def cond(st, n_low=n_low):
            j, least_ = st
            return (j < n_low) & (least_ <= SB_SKIP_LOG)

        def body(st, own=own, step=step):
            j, _ = st
            return j + 1, step(pl.multiple_of(own - j * SB_CHUNK, BLOCK), None)

        j_end, least = lax.while_loop(cond, body, (jnp.int32(2), least0))

        @pl.when((n_low >= 2) & (j_end == n_low) & (least <= SB_SKIP_LOG))
        def _(own=own, n_low=n_low, step=step):
            step(pl.multiple_of(own - n_low * SB_CHUNK, BLOCK), "low")

        acc = acc_ref[u]
        o_ref[u * BLOCK:(u + 1) * BLOCK, :] = jnp.where(lo, acc[0:BLOCK], acc[BLOCK:2 * BLOCK])


def _sb_attention(proj, meta_blk, batch, seq):
    nq = seq // BLOCK
    ng = nq // SB_QB
    n_pairs = SB_WIDTH // LANES
    rows = SB_QB * BLOCK
    return pl.pallas_call(
        functools.partial(_sb_body, seq=seq),
        grid=(batch, n_pairs, ng),
        in_specs=[
            pl.BlockSpec((rows, LANES), lambda b, h, g: (b * ng + g, COL_Q_SB + h)),
            pl.BlockSpec((seq, LANES), lambda b, h, g: (b, COL_K_SB + h)),
            pl.BlockSpec((seq, LANES), lambda b, h, g: (b, COL_V_SB + h)),
            pl.BlockSpec((BLOCK, LANES), lambda b, h, g: (0, COL_K_SB + h)),
            pl.BlockSpec((BLOCK, LANES), lambda b, h, g: (0, COL_V_SB + h)),
        ],
        out_specs=pl.BlockSpec((rows, LANES), lambda b, h, g: (b * ng + g, h)),
        out_shape=jax.ShapeDtypeStruct((batch * seq, SB_WIDTH), F32),
        scratch_shapes=[
            pltpu.VMEM((seq + 2 * BLOCK, LANES), BF16),
            pltpu.VMEM((seq + 2 * BLOCK, LANES), BF16),
            pltpu.VMEM((SB_QB, 2 * BLOCK, LANES), F32),
            pltpu.VMEM((SB_QB, 2 * BLOCK, LANES), F32),
        ],
        compiler_params=_cparams(("parallel", "parallel", "arbitrary"), 32),
        name="sb_attn",
    )(proj, proj, proj, meta_blk, meta_blk)


SWA_QB = 4


def _swa_body(q_ref, kp_ref, kc_ref, vp_ref, vc_ref, km_ref, vm_ref, bkt_ref,
              rel_ref, sink_ref, o_ref, bias_ref):
    t = pl.program_id(1)
    g = pl.program_id(2)
    h0 = 2 * t
    kv_in_hi = (lax.shift_right_logical(t, 1) & 1) == 1

    @pl.when(g == 0)
    def _():
        for tb in range(2):
            bkt = bkt_ref[tb]
            for hh in range(2):
                bias = jnp.full(bkt.shape, NEG_INF, F32)
                for b in range(NUM_BUCKETS):
                    bias = jnp.where(bkt == b, rel_ref[b, h0 + hh], bias)
                bias_ref[tb, hh * BLOCK:(hh + 1) * BLOCK, :] = bias

    lane = lax.broadcasted_iota(I32, (BLOCK, LANES), 1)
    lo = lane < HEAD_DIM
    kmask = jnp.logical_xor(lo, kv_in_hi)
    rcol = lax.broadcasted_iota(I32, (2 * BLOCK, 1), 0)
    sink = jnp.where(rcol < BLOCK, sink_ref[h0], sink_ref[h0 + 1])
    k_meta = km_ref[...]
    v_meta = vm_ref[...]

    blocks = range(SWA_QB)
    rows = [slice(u * BLOCK, (u + 1) * BLOCK) for u in blocks]
    scores, values = [], []
    for u in blocks:
        q = q_ref[rows[u], :]
        q_sw = jnp.concatenate([q[:, HEAD_DIM:], q[:, :HEAD_DIM]], axis=1)
        zq = jnp.zeros_like(q)
        qa = jnp.where(kmask, jnp.where(kv_in_hi, q_sw, q), zq)
        qb = jnp.where(kmask, jnp.where(kv_in_hi, q, q_sw), zq)
        qm = jnp.concatenate([qa, qb], axis=0)
        k_prev = kp_ref[...] if u == 0 else kc_ref[rows[u - 1], :]
        v_prev = vp_ref[...] if u == 0 else vc_ref[rows[u - 1], :]
        k_all = jnp.concatenate([k_prev, kc_ref[rows[u], :], k_meta], axis=0)
        values.append(jnp.concatenate([v_prev, vc_ref[rows[u], :], v_meta], axis=0))
        bias = bias_ref[jnp.minimum(g, 1)] if u == 0 else bias_ref[1]
        scores.append(lax.dot_general(qm, k_all, (((1,), (1,)), ((), ())), preferred_element_type=F32) + bias)
    peaks = [jnp.maximum(jnp.max(s, axis=-1, keepdims=True), sink) for s in scores]
    probs = [jnp.exp(s - m) for s, m in zip(scores, peaks)]
    denoms = [jnp.sum(p, axis=-1, keepdims=True) + jnp.exp(sink - m) for p, m in zip(probs, peaks)]
    outs = [jnp.dot((p * (1.0 / d)).astype(BF16), v, preferred_element_type=F32)
            for p, d, v in zip(probs, denoms, values)]
    for u in blocks:
        oa = outs[u][0:BLOCK]
        ob = outs[u][BLOCK:2 * BLOCK]
        oa_r = pltpu.roll(oa, HEAD_DIM, 1)
        ob_r = pltpu.roll(ob, HEAD_DIM, 1)
        o_ref[rows[u], :] = jnp.where(lo, jnp.where(kv_in_hi, oa_r, oa), jnp.where(kv_in_hi, ob, ob_r))


def _swa_attention(proj, meta_blk, bkt, rel_bias, sinks, batch, seq):
    nq = seq // BLOCK
    ng = nq // SWA_QB
    n_pairs = SWA_WIDTH // LANES
    rows = SWA_QB * BLOCK

    def cur(base):
        return lambda b, t, g: (b * ng + g, base + t // 4)

    def prev(base):
        return lambda b, t, g: (b * nq + jnp.maximum(SWA_QB * g - 1, 0), base + t // 4)

    smem = pl.BlockSpec(memory_space=pltpu.SMEM)
    return pl.pallas_call(
        _swa_body,
        grid=(batch, n_pairs, ng),
        in_specs=[
            pl.BlockSpec((rows, LANES), lambda b, t, g: (b * ng + g, COL_Q_SW + t)),
            pl.BlockSpec((BLOCK, LANES), prev(COL_K_SW)),
            pl.BlockSpec((rows, LANES), cur(COL_K_SW)),
            pl.BlockSpec((BLOCK, LANES), prev(COL_V_SW)),
            pl.BlockSpec((rows, LANES), cur(COL_V_SW)),
            pl.BlockSpec((BLOCK, LANES), lambda b, t, g: (0, COL_K_SW + t // 4)),
            pl.BlockSpec((BLOCK, LANES), lambda b, t, g: (0, COL_V_SW + t // 4)),
            pl.BlockSpec((2, BLOCK, 3 * BLOCK), lambda b, t, g: (0, 0, 0)),
            smem,
            smem,
        ],
        out_specs=pl.BlockSpec((rows, LANES), lambda b, t, g: (b * ng + g, t)),
        out_shape=jax.ShapeDtypeStruct((batch * seq, SWA_WIDTH), F32),
        scratch_shapes=[pltpu.VMEM((2, 2 * BLOCK, 3 * BLOCK), F32)],
        compiler_params=_cparams(("parallel", "parallel", "arbitrary"), 32),
        name="swa_attn",
    )(proj, proj, proj, proj, proj, meta_blk, meta_blk, bkt, rel_bias, sinks)


def _t5_bucket_np(dist):
    max_exact = NUM_BUCKETS // 2
    d = np.maximum(dist, 0)
    ratio = np.maximum(d, 1).astype(np.float32) / np.float32(max_exact)
    large = max_exact + (np.log(ratio) / np.float32(math.log(MAX_DISTANCE / max_exact))
                         * np.float32(NUM_BUCKETS - max_exact)).astype(np.int32)
    large = np.minimum(large, NUM_BUCKETS - 1)
    return np.where(d < max_exact, d, large).astype(np.int32)


def _bucket_tables():
    ql = np.arange(BLOCK)[:, None]
    sl = np.arange(2 * BLOCK)[None, :]
    dist = ql + BLOCK - sl
    ok = (dist >= 0) & (dist < WINDOW)
    band = np.where(ok, _t5_bucket_np(dist), -1).astype(np.int32)
    band_first = np.where(sl >= BLOCK, band, -1).astype(np.int32)
    ms = np.arange(BLOCK)[None, :]
    tables = []
    for blk, b in ((0, band_first), (1, band)):
        dist_m = ql + (blk + 1) * BLOCK - ms
        meta = np.where(ms >= META_PAD, _t5_bucket_np(dist_m), -1).astype(np.int32)
        tables.append(np.concatenate([b, meta], axis=1))
    return jnp.asarray(np.stack(tables))


def _out_router_body(osb_ref, osw_ref, x_ref, gsb_ref, gsw_ref, wo_ref, lf_ref, wr_ref, br_ref,
                     h1_ref, hfp_ref, route_ref):
    tm = x_ref.shape[0]
    n_groups = max(1, tm // TOK_TILE)
    rows = [slice(r * (tm // n_groups), (r + 1) * (tm // n_groups)) for r in range(n_groups)]

    def branch_norm(o_ref_, g_ref_, rs):
        a = o_ref_[rs, :]
        var = jnp.mean(a * a, axis=-1, keepdims=True)
        return ((a * lax.rsqrt(var + RMS_EPS)) * g_ref_[...]).astype(BF16)

    mixed = [jnp.concatenate([branch_norm(osb_ref, gsb_ref, rs), branch_norm(osw_ref, gsw_ref, rs)], axis=1)
             for rs in rows]
    h1s = [x_ref[rs, :] + jnp.dot(mx, wo_ref[...], preferred_element_type=F32) for mx, rs in zip(mixed, rows)]
    hfs = []
    for h1, rs in zip(h1s, rows):
        h1_ref[rs, :] = h1
        var = jnp.mean(h1 * h1, axis=-1, keepdims=True)
        hf = ((h1 * lax.rsqrt(var + RMS_EPS)) * lf_ref[...]).astype(BF16)
        hfs.append(hf)
        half = D_MODEL // 2
        lo_bits = pltpu.bitcast(hf[:, :half].astype(F32), U32)
        hi_bits = pltpu.bitcast(hf[:, half:].astype(F32), U32)
        hfp_ref[rs, :] = (hi_bits & jnp.uint32(0xFFFF0000)) | (lo_bits >> 16)
    all_logits = [jnp.dot(hf, wr_ref[...], preferred_element_type=F32) + br_ref[...] for hf in hfs]

    for logits, rs in zip(all_logits, rows):
        lane = lax.broadcasted_iota(I32, logits.shape, 1)
        big = jnp.int32(LANES)

        def softmax_masked(mask, logits=logits):
            z = jnp.where(mask, logits, NEG_INF)
            zmax = jnp.max(z, axis=-1, keepdims=True)
            e = jnp.exp(z - zmax)
            return e / jnp.sum(e, axis=-1, keepdims=True)

        def top1(p, mask, lane=lane):
            pm = jnp.where(mask, p, -1.0)
            v = jnp.max(pm, axis=-1, keepdims=True)
            idx = jnp.min(jnp.where(pm == v, lane, big), axis=-1, keepdims=True)
            return v, idx

        gmask = lane < N_GROUPS
        g_val, g_idx = top1(softmax_masked(gmask), gmask)
        e_lo = N_GROUPS + EXPERTS_PER_GROUP * g_idx
        emask = (lane >= e_lo) & (lane < e_lo + EXPERTS_PER_GROUP)
        e_prob = softmax_masked(emask)
        v1, i1 = top1(e_prob, emask)
        emask2 = emask & (lane != i1)
        v2, i2 = top1(e_prob, emask2)
        norm = v1 + v2
        gate1 = g_val * v1 / norm
        gate2 = g_val * v2 / norm
        ex1 = (i1 - N_GROUPS).astype(F32)
        ex2 = (i2 - N_GROUPS).astype(F32)
        route_ref[rs, :] = jnp.where(lane == 0, gate1,
                                     jnp.where(lane == 1, gate2,
                                               jnp.where(lane == 2, ex1,
                                                         jnp.where(lane == 3, ex2, 0.0))))


def _out_router(o_sb, o_sw, x2d, g_sb, g_sw, wo_bf, ln_ffn, w_router, b_router):
    m = x2d.shape[0]
    tm = min(OUT_TM, m)
    row = lambda w: pl.BlockSpec((tm, w), lambda i: (i, 0))
    const = lambda r, c: pl.BlockSpec((r, c), lambda i: (0, 0))
    return pl.pallas_call(
        _out_router_body,
        grid=(m // tm,),
        in_specs=[row(SB_WIDTH), row(SWA_WIDTH), row(D_MODEL), const(1, SB_WIDTH), const(1, SWA_WIDTH),
                  const(D_MODEL, D_MODEL), const(1, D_MODEL), const(D_MODEL, LANES), const(1, LANES)],
        out_specs=[row(D_MODEL), row(D_MODEL // 2), row(LANES)],
        out_shape=[jax.ShapeDtypeStruct((m, D_MODEL), F32),
                   jax.ShapeDtypeStruct((m, D_MODEL // 2), U32),
                   jax.ShapeDtypeStruct((m, LANES), F32)],
        compiler_params=_cparams(("parallel",), 56),
        name="out_router",
    )(o_sb, o_sw, x2d, g_sb, g_sw, wo_bf, ln_ffn, w_router, b_router)


W_DMA_PRIORITY = 1


def _experts_body(te_ref, nx_ref, ws_ref, base_ref, nu_ref, tok_ref, hfp_ref, wg_ref, wu_ref, wd_ref, y_ref,
                  xbuf, wg_f, wu_f, wd_f, wg_bf, wu_bf, wd_bf, gsem, wsem):
    n = pl.program_id(0)
    n_used = nu_ref[0]

    def row_copy(tile, r, slot):
        tok = tok_ref[base_ref[tile] + r]
        return pltpu.make_async_copy(hfp_ref.at[pl.ds(tok, 1), :], xbuf.at[slot, pl.ds(r, 1), :], gsem.at[slot])

    def row_wait(r, slot):
        pltpu.make_async_copy(hfp_ref.at[pl.ds(0, 1), :], xbuf.at[slot, pl.ds(r, 1), :], gsem.at[slot]).wait()

    def weight_copies(e, slot):
        return (pltpu.make_async_copy(wg_ref.at[e], wg_f.at[slot], wsem.at[slot, 0]),
                pltpu.make_async_copy(wu_ref.at[e], wu_f.at[slot], wsem.at[slot, 1]),
                pltpu.make_async_copy(wd_ref.at[e], wd_f.at[slot], wsem.at[slot, 2]))

    @pl.when(n == 0)
    def _():
        for c in weight_copies(te_ref[0], 0):
            c.start(priority=W_DMA_PRIORITY)
        for r in range(MOE_TILE):
            row_copy(0, r, 0).start()

    @pl.when(n < n_used)
    def _():
        prev = te_ref[jnp.maximum(n - 1, 0)]
        fresh = (n == 0) | (te_ref[n] != prev)

        @pl.when(fresh)
        def _():
            ws = ws_ref[n]
            for c in weight_copies(te_ref[n], ws):
                c.wait()

            @pl.when(nx_ref[n] >= 0)
            def _():
                for c in weight_copies(nx_ref[n], 1 - ws):
                    c.start(priority=W_DMA_PRIORITY)

            wg_bf[...] = wg_f[ws].astype(BF16)
            wu_bf[...] = wu_f[ws].astype(BF16)
            wd_bf[...] = wd_f[ws].astype(BF16)

        slot = n & 1
        for r in range(MOE_TILE):
            row_copy(n + 1, r, 1 - slot).start()
        for r in range(MOE_TILE):
            row_wait(r, slot)

        u = xbuf[slot]
        x_lo = pltpu.bitcast(u << 16, F32).astype(BF16)
        x_hi = pltpu.bitcast(u & jnp.uint32(0xFFFF0000), F32).astype(BF16)
        x = jnp.concatenate([x_lo, x_hi], axis=1)
        g = jnp.dot(x, wg_bf[...], preferred_element_type=F32)
        up = jnp.dot(x, wu_bf[...], preferred_element_type=F32)
        hdn = (g * jax.nn.sigmoid(g) * up).astype(BF16)
        y = jnp.dot(hdn, wd_bf[...], preferred_element_type=F32).astype(BF16)
        half = D_MODEL // 2
        y_lo = pltpu.bitcast(y[:, :half].astype(F32), U32)
        y_hi = pltpu.bitcast(y[:, half:].astype(F32), U32)
        y_ref[...] = (y_hi & jnp.uint32(0xFFFF0000)) | (y_lo >> 16)

    @pl.when(n == n_used)
    def _():
        for r in range(MOE_TILE):
            row_wait(r, n & 1)

    @pl.when(n >= n_used)
    def _():
        y_ref[...] = jnp.zeros_like(y_ref)


def _experts(tile_expert, next_expert, w_slot, tile_base, n_used, tok_sorted, hfp, w_gate, w_up, w_down):
    n_tiles = tile_expert.shape[0]
    any_space = pl.BlockSpec(memory_space=pl.ANY)
    grid_spec = pltpu.PrefetchScalarGridSpec(
        num_scalar_prefetch=6,
        grid=(n_tiles + 1,),
        in_specs=[any_space, any_space, any_space, any_space],
        out_specs=pl.BlockSpec((MOE_TILE, D_MODEL // 2), lambda n, *_: (n, 0)),
        scratch_shapes=[pltpu.VMEM((2, MOE_TILE, D_MODEL // 2), U32),
                        pltpu.VMEM((2, D_MODEL, D_EXPERT), F32),
                        pltpu.VMEM((2, D_MODEL, D_EXPERT), F32),
                        pltpu.VMEM((2, D_EXPERT, D_MODEL), F32),
                        pltpu.VMEM((D_MODEL, D_EXPERT), BF16),
                        pltpu.VMEM((D_MODEL, D_EXPERT), BF16),
                        pltpu.VMEM((D_EXPERT, D_MODEL), BF16),
                        pltpu.SemaphoreType.DMA((2,)),
                        pltpu.SemaphoreType.DMA((2, 3))],
    )
    return pl.pallas_call(
        _experts_body,
        grid_spec=grid_spec,
        out_shape=jax.ShapeDtypeStruct(((n_tiles + 1) * MOE_TILE, D_MODEL // 2), U32),
        compiler_params=_cparams(("arbitrary",), 56),
        name="moe_experts",
    )(tile_expert, next_expert, w_slot, tile_base, n_used, tok_sorted, hfp, w_gate, w_up, w_down)


def _combine_body(dcur_ref, dnxt_ref, h1_ref, route_ref, y_ref, o_ref, ybuf, sem, *, n_steps):
    i = pl.program_id(0)
    n = dcur_ref.shape[2]

    def row_copy(idx_ref, a, slot):
        return pltpu.make_async_copy(y_ref.at[pl.ds(idx_ref[0, 0, a], 1), :],
                                     ybuf.at[slot, a % 2, pl.ds(a // 2, 1), :], sem.at[slot])

    @pl.when(i == 0)
    def _():
        for a in range(n):
            row_copy(dcur_ref, a, 0).start()

    @pl.when(i < n_steps)
    def _():
        slot = i & 1
        for a in range(n):
            row_copy(dnxt_ref, a, 1 - slot).start(priority=a % 2)
        for a in range(n):
            row_copy(dcur_ref, a, slot).wait()
        route = route_ref[...]
        g0 = route[:, 0:1]
        g1 = route[:, 1:2]
        u0 = ybuf[slot, 0]
        u1 = ybuf[slot, 1]
        half = D_MODEL // 2
        hi_mask = jnp.uint32(0xFFFF0000)
        o_ref[:, :half] = h1_ref[:, :half] + (pltpu.bitcast(u0 << 16, F32) * g0 + pltpu.bitcast(u1 << 16, F32) * g1)
        o_ref[:, half:] = h1_ref[:, half:] + (pltpu.bitcast(u0 & hi_mask, F32) * g0
                                              + pltpu.bitcast(u1 & hi_mask, F32) * g1)

    @pl.when(i == n_steps)
    def _():
        for a in range(n):
            row_copy(dcur_ref, a, i & 1).wait()


def _combine(dest3, h1, route, y):
    m = h1.shape[0]
    tm = dest3.shape[2] // 2
    n_steps = m // tm
    tok = lambda i: (jnp.minimum(i, n_steps - 1), 0)
    return pl.pallas_call(
        functools.partial(_combine_body, n_steps=n_steps),
        grid=(n_steps + 1,),
        in_specs=[
            pl.BlockSpec((1, 1, 2 * tm), lambda i: (i, 0, 0), memory_space=pltpu.SMEM),
            pl.BlockSpec((1, 1, 2 * tm), lambda i: (jnp.minimum(i + 1, n_steps), 0, 0), memory_space=pltpu.SMEM),
            pl.BlockSpec((tm, D_MODEL), tok),
            pl.BlockSpec((tm, LANES), tok),
            pl.BlockSpec(memory_space=pl.ANY),
        ],
        out_specs=pl.BlockSpec((tm, D_MODEL), tok),
        out_shape=jax.ShapeDtypeStruct((m, D_MODEL), F32),
        scratch_shapes=[pltpu.VMEM((2, 2, tm, D_MODEL // 2), U32), pltpu.SemaphoreType.DMA((2,))],
        compiler_params=_cparams(("arbitrary",), 40),
        name="moe_combine",
    )(dest3, dest3, h1, route, y)


def _routing_tables(route, n_tok):
    a_expert = route[:, 2:4].astype(I32).reshape(-1)
    n_assign = a_expert.shape[0]
    eids = jnp.arange(N_EXPERTS, dtype=I32)
    sorted_e, order = lax.sort((a_expert, jnp.arange(n_assign, dtype=I32)), num_keys=1, is_stable=True)
    first = jnp.sum((sorted_e[None, :] < eids[:, None]).astype(I32), axis=1)
    counts = jnp.concatenate([first[1:], jnp.full((1,), n_assign, I32)]) - first
    padded = (counts + MOE_TILE - 1) // MOE_TILE * MOE_TILE
    upto = (eids[None, :] <= eids[:, None]).astype(I32)
    seg_end = jnp.sum(upto * padded[None, :], axis=1)
    seg_start = seg_end - padded
    shift = seg_start - first
    shift_p = jnp.sum(jnp.where(sorted_e[None, :] == eids[:, None], shift[:, None], 0), axis=0)
    _, dest = lax.sort((order, jnp.arange(n_assign, dtype=I32) + shift_p), num_keys=1)

    n_tiles = n_assign // MOE_TILE + N_EXPERTS
    n_used = (seg_end[-1] // MOE_TILE).astype(I32)
    tile_start = jnp.arange(n_tiles + 2, dtype=I32) * MOE_TILE
    te = jnp.sum((seg_end[None, :] <= tile_start[:, None]).astype(I32), axis=1)
    te = jnp.minimum(te, N_EXPERTS - 1).astype(I32)
    used = counts > 0
    later_used = used[None, :] & (eids[None, :] > eids[:, None])
    next_e = jnp.min(jnp.where(later_used, eids[None, :], N_EXPERTS), axis=1)
    next_e = jnp.where(next_e == N_EXPERTS, -1, next_e).astype(I32)
    ordinal = jnp.sum(upto * used.astype(I32)[None, :], axis=1) - 1
    base = first[te] + tile_start - seg_start[te]
    base = jnp.where(jnp.arange(n_tiles + 2) < n_used, jnp.clip(base, 0, n_assign), n_assign).astype(I32)
    tok_sorted = jnp.concatenate([order // 2, jnp.zeros((MOE_TILE,), I32)])
    return (dest.astype(I32), tok_sorted, base, te[:n_tiles], next_e[te[:n_tiles]],
            (ordinal[te[:n_tiles]] & 1).astype(I32), n_used.reshape(1))


def kernel(x, meta_tokens, rel_bias, ln_mix, w_in, q_norm, k_norm, sinks, out_norm_sb, out_norm_swa,
           w_out, ln_ffn, w_router_group, b_router_group, w_router_expert, b_router_expert,
           w_gate, w_up, w_down):
    batch, seq, _ = x.shape
    n_tok = batch * seq
    x2d = x.reshape(n_tok, D_MODEL)

    ones = lambda n: jnp.ones((n,), F32)
    col_scale = jnp.concatenate([
        ones(SB_WIDTH) * SCALE, ones(2 * SB_WIDTH),
        jnp.tile(q_norm[0], SWA_Q_HEADS) * SCALE, jnp.tile(k_norm[0], SWA_KV_HEADS), ones(SWA_KV_WIDTH)])[None]
    norm_flag = jnp.concatenate([jnp.zeros((3 * SB_WIDTH,), F32), ones(SWA_WIDTH + SWA_KV_WIDTH),
                                 jnp.zeros((SWA_KV_WIDTH,), F32)])[None]
    w_in_bf = w_in[0].astype(BF16)
    gain_mix = ln_mix[0][None]

    proj = _norm_proj(x2d, gain_mix, w_in_bf, col_scale, norm_flag)
    meta_proj = _norm_proj(meta_tokens.astype(F32), gain_mix, w_in_bf, col_scale, norm_flag)
    meta_blk = jnp.concatenate([jnp.zeros((META_PAD, IN_PROJ_WIDTH), BF16), meta_proj], axis=0)

    o_sb = _sb_attention(proj, meta_blk, batch, seq)
    o_sw = _swa_attention(proj, meta_blk, _bucket_tables(), rel_bias.astype(F32), sinks[0].astype(F32),
                          batch, seq)

    w_router = jnp.concatenate(
        [w_router_group[0], w_router_expert[0],
         jnp.zeros((D_MODEL, LANES - N_GROUPS - N_EXPERTS), F32)], axis=1).astype(BF16)
    b_router = jnp.concatenate(
        [b_router_group[0], b_router_expert[0], jnp.zeros((LANES - N_GROUPS - N_EXPERTS,), F32)])[None]
    h1, hfp, route = _out_router(o_sb, o_sw, x2d, out_norm_sb[0][None], out_norm_swa[0][None],
                                 w_out[0].astype(BF16), ln_ffn[0][None], w_router, b_router)

    dest, tok_sorted, tile_base, tile_expert, next_expert, w_slot, n_used = _routing_tables(route, n_tok)
    y = _experts(tile_expert, next_expert, w_slot, tile_base, n_used, tok_sorted, hfp,
                 w_gate[0], w_up[0], w_down[0])
    dest3 = jnp.concatenate([dest, jnp.zeros((2 * TOK_TILE,), I32)]).reshape(-1, 1, 2 * TOK_TILE)
    out = _combine(dest3, h1, route, y)
    return out.reshape(batch, seq, D_MODEL)
```

```python
import functools
import math

import numpy as np
import jax
import jax.numpy as jnp
from jax import lax
from jax.experimental import pallas as pl
from jax.experimental.pallas import tpu as pltpu

F32 = jnp.float32
BF16 = jnp.bfloat16
I32 = jnp.int32
U32 = jnp.uint32

D_MODEL = 2048
HEAD_DIM = 64
LANES = 128
SB_HEADS = 16
SWA_Q_HEADS = 16
SWA_KV_HEADS = 4
SB_WIDTH = SB_HEADS * HEAD_DIM
SWA_WIDTH = SWA_Q_HEADS * HEAD_DIM
SWA_KV_WIDTH = SWA_KV_HEADS * HEAD_DIM
IN_PROJ_WIDTH = 3 * SB_WIDTH + SWA_WIDTH + 2 * SWA_KV_WIDTH
BLOCK = 128
WINDOW = 128
N_META = 16
META_PAD = BLOCK - N_META
NUM_BUCKETS = 32
MAX_DISTANCE = 128
N_GROUPS = 8
EXPERTS_PER_GROUP = 8
N_EXPERTS = N_GROUPS * EXPERTS_PER_GROUP
D_EXPERT = D_MODEL // 4
RMS_EPS = 1e-6
SCALE = HEAD_DIM ** -0.5
LOG2E = 1.4426950408889634

COL_Q_SB = 0
COL_K_SB = SB_WIDTH // LANES
COL_V_SB = 2 * SB_WIDTH // LANES
COL_Q_SW = 3 * SB_WIDTH // LANES
COL_K_SW = COL_Q_SW + SWA_WIDTH // LANES
COL_V_SW = COL_K_SW + SWA_KV_WIDTH // LANES

PROJ_TN = 768
PROJ_TM = 512
OUT_TM = 512
SB_CHUNK = 256
MOE_TILE = 256
TOK_TILE = 256
NEG_INF = float("-inf")


def _cparams(sem, vmem_mb):
    return pltpu.CompilerParams(dimension_semantics=sem, vmem_limit_bytes=vmem_mb * 1024 * 1024)


def _norm_proj_body(x_ref, g_ref, w_ref, cs_ref, nf_ref, o_ref, *, n_plain_tiles):
    x = x_ref[...]
    var = jnp.mean(x * x, axis=-1, keepdims=True)
    hn = (x * lax.rsqrt(var + RMS_EPS) * g_ref[...]).astype(BF16)
    lane = lax.broadcasted_iota(I32, (1, LANES), 1)
    lo = lane < HEAD_DIM
    for j in range(w_ref.shape[1] // PROJ_TN):
        cols = slice(j * PROJ_TN, (j + 1) * PROJ_TN)
        acc = jnp.dot(hn, w_ref[:, cols], preferred_element_type=F32)
        if j < n_plain_tiles:
            o_ref[:, cols] = (acc * cs_ref[:, cols]).astype(o_ref.dtype)
            continue
        for c in range(PROJ_TN // LANES):
            sl = slice(j * PROJ_TN + c * LANES, j * PROJ_TN + (c + 1) * LANES)
            a = acc[:, c * LANES:(c + 1) * LANES]
            s = a * a
            s_lo = jnp.sum(jnp.where(lo, s, 0.0), axis=-1, keepdims=True)
            s_hi = jnp.sum(jnp.where(lo, 0.0, s), axis=-1, keepdims=True)
            ms = jnp.where(lo, s_lo, s_hi) * (1.0 / HEAD_DIM)
            r = jnp.where(nf_ref[:, sl] > 0.0, lax.rsqrt(ms + RMS_EPS), 1.0)
            o_ref[:, sl] = ((a * r) * cs_ref[:, sl]).astype(o_ref.dtype)


def _norm_proj(x2d, gain, w_bf, col_scale, norm_flag):
    m = x2d.shape[0]
    tm = min(PROJ_TM, m)
    n = w_bf.shape[1]
    n_plain = (3 * SB_WIDTH) // PROJ_TN
    whole = lambda r, c: pl.BlockSpec((r, c), lambda i: (0, 0))
    return pl.pallas_call(
        functools.partial(_norm_proj_body, n_plain_tiles=n_plain),
        grid=(m // tm,),
        in_specs=[pl.BlockSpec((tm, D_MODEL), lambda i: (i, 0)), whole(1, D_MODEL), whole(D_MODEL, n),
                  whole(1, n), whole(1, n)],
        out_specs=pl.BlockSpec((tm, n), lambda i: (i, 0)),
        out_shape=jax.ShapeDtypeStruct((m, n), BF16),
        compiler_params=_cparams(("parallel",), 56),
        name="norm_proj",
    )(x2d, gain, w_bf, col_scale, norm_flag)


SB_SKIP_LOG = 104.0


SB_QB = 8


def _sb_body(q_ref, k_ref, v_ref, km_ref, vm_ref, o_ref, kx_ref, vx_ref, acc_ref, car_ref, *, seq):
    g = pl.program_id(2)

    @pl.when(g == 0)
    def _():
        zeros = jnp.zeros((BLOCK, LANES), BF16)
        kx_ref[0:BLOCK, :] = zeros
        vx_ref[0:BLOCK, :] = zeros
        kx_ref[BLOCK:2 * BLOCK, :] = km_ref[...]
        vx_ref[BLOCK:2 * BLOCK, :] = vm_ref[...]
        kx_ref[2 * BLOCK:, :] = k_ref[...]
        vx_ref[2 * BLOCK:, :] = v_ref[...]

    lane = lax.broadcasted_iota(I32, (BLOCK, LANES), 1)
    lo = lane < HEAD_DIM

    def later_key_matrix(nk):
        r = lax.broadcasted_iota(I32, (nk, nk), 0)
        c = lax.broadcasted_iota(I32, (nk, nk), 1)
        return (r > c).astype(F32).astype(BF16)

    tri = {BLOCK: later_key_matrix(BLOCK), SB_CHUNK: later_key_matrix(SB_CHUNK)}

    def stage_qk(qm, start, nk):
        kc = kx_ref[pl.ds(start, nk), :]
        return lax.dot_general(qm, kc, (((1,), (1,)), ((), ())), preferred_element_type=F32)

    def stage_soft(s, start, mask):
        nk = s.shape[1]
        sp = jnp.maximum(s, 0.0) + jnp.log(1.0 + jnp.exp2(jnp.abs(s) * (-LOG2E)))
        log_beta = s - sp
        valid = None
        if mask is not None:
            r_i = lax.broadcasted_iota(I32, (2 * BLOCK, nk), 0)
            c_i = lax.broadcasted_iota(I32, (2 * BLOCK, nk), 1)
            if mask == "diag":
                valid = c_i < (r_i & (BLOCK - 1))
                sp = jnp.where(valid, sp, 0.0)
            else:
                sp = jnp.where((start + c_i) >= (BLOCK + META_PAD), sp, 0.0)
        return sp, log_beta, valid

    def stage_inner(soft, start):
        sp, log_beta, valid = soft
        nk = sp.shape[1]
        inner = jnp.dot(sp.astype(BF16), tri[nk], preferred_element_type=F32)
        return log_beta - inner, jnp.sum(sp, axis=1, keepdims=True), valid, vx_ref[pl.ds(start, nk), :]

    def scores(qm, start, nk, mask):
        return stage_inner(stage_soft(stage_qk(qm, start, nk), start, mask), start)

    def weights(part, car):
        log_w0, _, valid, _ = part
        nk = log_w0.shape[1]
        car_k = car if nk == LANES else jnp.concatenate([car, car], axis=1)
        w = jnp.exp(log_w0 - car_k)
        if valid is not None:
            w = jnp.where(valid, w, 0.0)
        return w.astype(BF16)

    def least_of(car):
        return jnp.min(jnp.min(car, axis=0, keepdims=True))

    idx, owns, qms = [], [], []
    for u in range(SB_QB):
        i = SB_QB * g + u
        idx.append(i)
        owns.append(pl.multiple_of(BLOCK * (i + 2), BLOCK))
        q = q_ref[u * BLOCK:(u + 1) * BLOCK, :]
        zq = jnp.zeros_like(q)
        qms.append(jnp.concatenate([jnp.where(lo, q, zq), jnp.where(lo, zq, q)], axis=0))
    chunks = []
    for u in range(SB_QB):
        chunks.append((u, owns[u], BLOCK, "diag"))
        chunks.append((u, pl.multiple_of(owns[u] - SB_CHUNK, BLOCK), SB_CHUNK, "low"))
    qk = [stage_qk(qms[u], start, nk) for u, start, nk, _ in chunks]
    soft = [stage_soft(s, start, mask) for s, (_, start, _, mask) in zip(qk, chunks)]
    parts = [stage_inner(sf, start) for sf, (_, start, _, _) in zip(soft, chunks)]
    tiles = []
    for u in range(SB_QB):
        diag, low1 = parts[2 * u], parts[2 * u + 1]
        car = jnp.broadcast_to(diag[1], (2 * BLOCK, LANES))
        acc = jnp.dot(jnp.where(diag[2], jnp.exp(diag[0]), 0.0).astype(BF16), diag[3],
                      preferred_element_type=F32)
        acc = acc + jnp.dot(weights(low1, car), low1[3], preferred_element_type=F32)
        car = car + low1[1]
        acc_ref[u] = acc
        car_ref[u] = car
        tiles.append((idx[u], owns[u], qms[u], least_of(car)))

    for u, (i, own, qm, least0) in enumerate(tiles):
        n_low = lax.shift_right_logical(i + 2, 1)

        def step(start, mask, u=u, qm=qm):
            part = scores(qm, start, SB_CHUNK, mask)
            car_ = car_ref[u]
            acc_ref[u] += jnp.dot(weights(part, car_), part[3], preferred_element_type=F32)
            new_car = car_ + part[1]
            car_ref[u] = new_car
            return least_of(new_car)

        def cond(st, n_low=n_low):
            j, least_ = st
            return (j < n_low) & (least_ <= SB_SKIP_LOG)

        def body(st, own=own, step=step):
            j, _ = st
            return j + 1, step(pl.multiple_of(own - j * SB_CHUNK, BLOCK), None)

        j_end, least = lax.while_loop(cond, body, (jnp.int32(2), least0))

        @pl.when((n_low >= 2) & (j_end == n_low) & (least <= SB_SKIP_LOG))
        def _(own=own, n_low=n_low, step=step):
            step(pl.multiple_of(own - n_low * SB_CHUNK, BLOCK), "low")

        acc = acc_ref[u]
        o_ref[u * BLOCK:(u + 1) * BLOCK, :] = jnp.where(lo, acc[0:BLOCK], acc[BLOCK:2 * BLOCK])


def _sb_attention(proj, meta_blk, batch, seq):
    nq = seq // BLOCK
    ng = nq // SB_QB
    n_pairs = SB_WIDTH // LANES
    rows = SB_QB * BLOCK
    return pl.pallas_call(
        functools.partial(_sb_body, seq=seq),
        grid=(batch, n_pairs, ng),
        in_specs=[
            pl.BlockSpec((rows, LANES), lambda b, h, g: (b * ng + g, COL_Q_SB + h)),
            pl.BlockSpec((seq, LANES), lambda b, h, g: (b, COL_K_SB + h)),
            pl.BlockSpec((seq, LANES), lambda b, h, g: (b, COL_V_SB + h)),
            pl.BlockSpec((BLOCK, LANES), lambda b, h, g: (0, COL_K_SB + h)),
            pl.BlockSpec((BLOCK, LANES), lambda b, h, g: (0, COL_V_SB + h)),
        ],
        out_specs=pl.BlockSpec((rows, LANES), lambda b, h, g: (b * ng + g, h)),
        out_shape=jax.ShapeDtypeStruct((batch * seq, SB_WIDTH), F32),
        scratch_shapes=[
            pltpu.VMEM((seq + 2 * BLOCK, LANES), BF16),
            pltpu.VMEM((seq + 2 * BLOCK, LANES), BF16),
            pltpu.VMEM((SB_QB, 2 * BLOCK, LANES), F32),
            pltpu.VMEM((SB_QB, 2 * BLOCK, LANES), F32),
        ],
        compiler_params=_cparams(("parallel", "parallel", "arbitrary"), 32),
        name="sb_attn",
    )(proj, proj, proj, meta_blk, meta_blk)


SWA_QB = 8


def _swa_body(q_ref, kp_ref, kc_ref, vp_ref, vc_ref, km_ref, vm_ref, bkt_ref,
              rel_ref, sink_ref, o_ref, bias_ref):
    t = pl.program_id(0)
    g = pl.program_id(2)
    h0 = 2 * t
    kv_in_hi = (lax.shift_right_logical(t, 1) & 1) == 1

    @pl.when((pl.program_id(1) == 0) & (g == 0))
    def _():
        for tb in range(2):
            bkt = bkt_ref[tb]
            for hh in range(2):
                bias = jnp.full(bkt.shape, NEG_INF, F32)
                for b in range(NUM_BUCKETS):
                    bias = jnp.where(bkt == b, rel_ref[b, h0 + hh], bias)
                bias_ref[tb, hh * BLOCK:(hh + 1) * BLOCK, :] = bias

    lane = lax.broadcasted_iota(I32, (BLOCK, LANES), 1)
    lo = lane < HEAD_DIM
    kmask = jnp.logical_xor(lo, kv_in_hi)
    rcol = lax.broadcasted_iota(I32, (2 * BLOCK, 1), 0)
    sink = jnp.where(rcol < BLOCK, sink_ref[h0], sink_ref[h0 + 1])
    k_meta = km_ref[...]
    v_meta = vm_ref[...]

    blocks = range(SWA_QB)
    rows = [slice(u * BLOCK, (u + 1) * BLOCK) for u in blocks]
    scores, values = [], []
    for u in blocks:
        q = q_ref[rows[u], :]
        q_sw = jnp.concatenate([q[:, HEAD_DIM:], q[:, :HEAD_DIM]], axis=1)
        zq = jnp.zeros_like(q)
        qa = jnp.where(kmask, jnp.where(kv_in_hi, q_sw, q), zq)
        qb = jnp.where(kmask, jnp.where(kv_in_hi, q, q_sw), zq)
        qm = jnp.concatenate([qa, qb], axis=0)
        k_prev = kp_ref[...] if u == 0 else kc_ref[rows[u - 1], :]
        v_prev = vp_ref[...] if u == 0 else vc_ref[rows[u - 1], :]
        k_all = jnp.concatenate([k_prev, kc_ref[rows[u], :], k_meta], axis=0)
        values.append(jnp.concatenate([v_prev, vc_ref[rows[u], :], v_meta], axis=0))
        bias = bias_ref[jnp.minimum(g, 1)] if u == 0 else bias_ref[1]
        scores.append(lax.dot_general(qm, k_all, (((1,), (1,)), ((), ())), preferred_element_type=F32) + bias)
    peaks = [jnp.maximum(jnp.max(s, axis=-1, keepdims=True), sink) for s in scores]
    probs = [jnp.exp(s - m) for s, m in zip(scores, peaks)]
    denoms = [jnp.sum(p, axis=-1, keepdims=True) + jnp.exp(sink - m) for p, m in zip(probs, peaks)]
    outs = [jnp.dot((p * (1.0 / d)).astype(BF16), v, preferred_element_type=F32)
            for p, d, v in zip(probs, denoms, values)]
    for u in blocks:
        oa = outs[u][0:BLOCK]
        ob = outs[u][BLOCK:2 * BLOCK]
        oa_r = pltpu.roll(oa, HEAD_DIM, 1)
        ob_r = pltpu.roll(ob, HEAD_DIM, 1)
        o_ref[rows[u], :] = jnp.where(lo, jnp.where(kv_in_hi, oa_r, oa), jnp.where(kv_in_hi, ob, ob_r))


def _swa_attention(proj, meta_blk, bkt, rel_bias, sinks, batch, seq):
    nq = seq // BLOCK
    ng = nq // SWA_QB
    n_pairs = SWA_WIDTH // LANES
    rows = SWA_QB * BLOCK

    def cur(base):
        return lambda t, b, g: (b * ng + g, base + t // 4)

    def prev(base):
        return lambda t, b, g: (b * nq + jnp.maximum(SWA_QB * g - 1, 0), base + t // 4)

    smem = pl.BlockSpec(memory_space=pltpu.SMEM)
    return pl.pallas_call(
        _swa_body,
        grid=(n_pairs, batch, ng),
        in_specs=[
            pl.BlockSpec((rows, LANES), lambda t, b, g: (b * ng + g, COL_Q_SW + t)),
            pl.BlockSpec((BLOCK, LANES), prev(COL_K_SW)),
            pl.BlockSpec((rows, LANES), cur(COL_K_SW)),
            pl.BlockSpec((BLOCK, LANES), prev(COL_V_SW)),
            pl.BlockSpec((rows, LANES), cur(COL_V_SW)),
            pl.BlockSpec((BLOCK, LANES), lambda t, b, g: (0, COL_K_SW + t // 4)),
            pl.BlockSpec((BLOCK, LANES), lambda t, b, g: (0, COL_V_SW + t // 4)),
            pl.BlockSpec((2, BLOCK, 3 * BLOCK), lambda t, b, g: (0, 0, 0)),
            smem,
            smem,
        ],
        out_specs=pl.BlockSpec((rows, LANES), lambda t, b, g: (b * ng + g, t)),
        out_shape=jax.ShapeDtypeStruct((batch * seq, SWA_WIDTH), F32),
        scratch_shapes=[pltpu.VMEM((2, 2 * BLOCK, 3 * BLOCK), F32)],
        compiler_params=_cparams(("arbitrary", "arbitrary", "arbitrary"), 32),
        name="swa_attn",
    )(proj, proj, proj, proj, proj, meta_blk, meta_blk, bkt, rel_bias, sinks)


def _t5_bucket_np(dist):
    max_exact = NUM_BUCKETS // 2
    d = np.maximum(dist, 0)
    ratio = np.maximum(d, 1).astype(np.float32) / np.float32(max_exact)
    large = max_exact + (np.log(ratio) / np.float32(math.log(MAX_DISTANCE / max_exact))
                         * np.float32(NUM_BUCKETS - max_exact)).astype(np.int32)
    large = np.minimum(large, NUM_BUCKETS - 1)
    return np.where(d < max_exact, d, large).astype(np.int32)


def _bucket_tables():
    ql = np.arange(BLOCK)[:, None]
    sl = np.arange(2 * BLOCK)[None, :]
    dist = ql + BLOCK - sl
    ok = (dist >= 0) & (dist < WINDOW)
    band = np.where(ok, _t5_bucket_np(dist), -1).astype(np.int32)
    band_first = np.where(sl >= BLOCK, band, -1).astype(np.int32)
    ms = np.arange(BLOCK)[None, :]
    tables = []
    for blk, b in ((0, band_first), (1, band)):
        dist_m = ql + (blk + 1) * BLOCK - ms
        meta = np.where(ms >= META_PAD, _t5_bucket_np(dist_m), -1).astype(np.int32)
        tables.append(np.concatenate([b, meta], axis=1))
    return jnp.asarray(np.stack(tables))


def _out_router_body(osb_ref, osw_ref, x_ref, gsb_ref, gsw_ref, wo_ref, lf_ref, wr_ref, br_ref,
                     h1_ref, hfp_ref, route_ref):
    tm = x_ref.shape[0]
    n_groups = max(1, tm // TOK_TILE)
    rows = [slice(r * (tm // n_groups), (r + 1) * (tm // n_groups)) for r in range(n_groups)]

    def branch_norm(o_ref_, g_ref_, rs):
        a = o_ref_[rs, :]
        var = jnp.mean(a * a, axis=-1, keepdims=True)
        return ((a * lax.rsqrt(var + RMS_EPS)) * g_ref_[...]).astype(BF16)

    mixed = [jnp.concatenate([branch_norm(osb_ref, gsb_ref, rs), branch_norm(osw_ref, gsw_ref, rs)], axis=1)
             for rs in rows]
    h1s = [x_ref[rs, :] + jnp.dot(mx, wo_ref[...], preferred_element_type=F32) for mx, rs in zip(mixed, rows)]
    hfs = []
    for h1, rs in zip(h1s, rows):
        h1_ref[rs, :] = h1
        var = jnp.mean(h1 * h1, axis=-1, keepdims=True)
        hf = ((h1 * lax.rsqrt(var + RMS_EPS)) * lf_ref[...]).astype(BF16)
        hfs.append(hf)
        half = D_MODEL // 2
        lo_bits = pltpu.bitcast(hf[:, :half].astype(F32), U32)
        hi_bits = pltpu.bitcast(hf[:, half:].astype(F32), U32)
        hfp_ref[rs, :] = (hi_bits & jnp.uint32(0xFFFF0000)) | (lo_bits >> 16)
    all_logits = [jnp.dot(hf, wr_ref[...], preferred_element_type=F32) + br_ref[...] for hf in hfs]

    for logits, rs in zip(all_logits, rows):
        lane = lax.broadcasted_iota(I32, logits.shape, 1)
        big = jnp.int32(LANES)

        def softmax_masked(mask, logits=logits):
            z = jnp.where(mask, logits, NEG_INF)
            zmax = jnp.max(z, axis=-1, keepdims=True)
            e = jnp.exp(z - zmax)
            return e / jnp.sum(e, axis=-1, keepdims=True)

        def top1(p, mask, lane=lane):
            pm = jnp.where(mask, p, -1.0)
            v = jnp.max(pm, axis=-1, keepdims=True)
            idx = jnp.min(jnp.where(pm == v, lane, big), axis=-1, keepdims=True)
            return v, idx

        gmask = lane < N_GROUPS
        g_val, g_idx = top1(softmax_masked(gmask), gmask)
        e_lo = N_GROUPS + EXPERTS_PER_GROUP * g_idx
        emask = (lane >= e_lo) & (lane < e_lo + EXPERTS_PER_GROUP)
        e_prob = softmax_masked(emask)
        v1, i1 = top1(e_prob, emask)
        emask2 = emask & (lane != i1)
        v2, i2 = top1(e_prob, emask2)
        norm = v1 + v2
        gate1 = g_val * v1 / norm
        gate2 = g_val * v2 / norm
        ex1 = (i1 - N_GROUPS).astype(F32)
        ex2 = (i2 - N_GROUPS).astype(F32)
        route_ref[rs, :] = jnp.where(lane == 0, gate1,
                                     jnp.where(lane == 1, gate2,
                                               jnp.where(lane == 2, ex1,
                                                         jnp.where(lane == 3, ex2, 0.0))))


def _out_router(o_sb, o_sw, x2d, g_sb, g_sw, wo_bf, ln_ffn, w_router, b_router):
    m = x2d.shape[0]
    tm = min(OUT_TM, m)
    row = lambda w: pl.BlockSpec((tm, w), lambda i: (i, 0))
    const = lambda r, c: pl.BlockSpec((r, c), lambda i: (0, 0))
    return pl.pallas_call(
        _out_router_body,
        grid=(m // tm,),
        in_specs=[row(SB_WIDTH), row(SWA_WIDTH), row(D_MODEL), const(1, SB_WIDTH), const(1, SWA_WIDTH),
                  const(D_MODEL, D_MODEL), const(1, D_MODEL), const(D_MODEL, LANES), const(1, LANES)],
        out_specs=[row(D_MODEL), row(D_MODEL // 2), row(LANES)],
        out_shape=[jax.ShapeDtypeStruct((m, D_MODEL), F32),
                   jax.ShapeDtypeStruct((m, D_MODEL // 2), U32),
                   jax.ShapeDtypeStruct((m, LANES), F32)],
        compiler_params=_cparams(("parallel",), 56),
        name="out_router",
    )(o_sb, o_sw, x2d, g_sb, g_sw, wo_bf, ln_ffn, w_router, b_router)


W_DMA_PRIORITY = 1


def _experts_body(te_ref, nx_ref, ws_ref, base_ref, nu_ref, tok_ref, hfp_ref, wg_ref, wu_ref, wd_ref, y_ref,
                  xbuf, wg_f, wu_f, wd_f, wg_bf, wu_bf, wd_bf, gsem, wsem):
    n = pl.program_id(0)
    n_used = nu_ref[0]

    def row_copy(tile, r, slot):
        tok = tok_ref[base_ref[tile] + r]
        return pltpu.make_async_copy(hfp_ref.at[pl.ds(tok, 1), :], xbuf.at[slot, pl.ds(r, 1), :], gsem.at[slot])

    def row_wait(r, slot):
        pltpu.make_async_copy(hfp_ref.at[pl.ds(0, 1), :], xbuf.at[slot, pl.ds(r, 1), :], gsem.at[slot]).wait()

    def weight_copies(e, slot):
        return (pltpu.make_async_copy(wg_ref.at[e], wg_f.at[slot], wsem.at[slot, 0]),
                pltpu.make_async_copy(wu_ref.at[e], wu_f.at[slot], wsem.at[slot, 1]),
                pltpu.make_async_copy(wd_ref.at[e], wd_f.at[slot], wsem.at[slot, 2]))

    @pl.when(n == 0)
    def _():
        for c in weight_copies(te_ref[0], 0):
            c.start(priority=W_DMA_PRIORITY)
        for r in range(MOE_TILE):
            row_copy(0, r, 0).start()

    @pl.when(n < n_used)
    def _():
        prev = te_ref[jnp.maximum(n - 1, 0)]
        fresh = (n == 0) | (te_ref[n] != prev)

        @pl.when(fresh)
        def _():
            ws = ws_ref[n]
            for c in weight_copies(te_ref[n], ws):
                c.wait()

            @pl.when(nx_ref[n] >= 0)
            def _():
                for c in weight_copies(nx_ref[n], 1 - ws):
                    c.start(priority=W_DMA_PRIORITY)

            wg_bf[...] = wg_f[ws].astype(BF16)
            wu_bf[...] = wu_f[ws].astype(BF16)
            wd_bf[...] = wd_f[ws].astype(BF16)

        def tile(slot):
            for r in range(MOE_TILE):
                row_copy(n + 1, r, 1 - slot).start()
            for r in range(MOE_TILE):
                row_wait(r, slot)

            u = xbuf[slot]
            x_lo = pltpu.bitcast(u << 16, F32).astype(BF16)
            x_hi = pltpu.bitcast(u & jnp.uint32(0xFFFF0000), F32).astype(BF16)
            x = jnp.concatenate([x_lo, x_hi], axis=1)
            g = jnp.dot(x, wg_bf[...], preferred_element_type=F32)
            up = jnp.dot(x, wu_bf[...], preferred_element_type=F32)
            hdn = (g * jax.nn.sigmoid(g) * up).astype(BF16)
            y = jnp.dot(hdn, wd_bf[...], preferred_element_type=F32).astype(BF16)
            half = D_MODEL // 2
            y_lo = pltpu.bitcast(y[:, :half].astype(F32), U32)
            y_hi = pltpu.bitcast(y[:, half:].astype(F32), U32)
            y_ref[...] = (y_hi & jnp.uint32(0xFFFF0000)) | (y_lo >> 16)

        for parity in range(2):
            pl.when((n & 1) == parity)(functools.partial(tile, parity))

    @pl.when(n == n_used)
    def _():
        for r in range(MOE_TILE):
            row_wait(r, n & 1)

    @pl.when(n >= n_used)
    def _():
        y_ref[...] = jnp.zeros_like(y_ref)


def _experts(tile_expert, next_expert, w_slot, tile_base, n_used, tok_sorted, hfp, w_gate, w_up, w_down):
    n_tiles = tile_expert.shape[0]
    any_space = pl.BlockSpec(memory_space=pl.ANY)
    grid_spec = pltpu.PrefetchScalarGridSpec(
        num_scalar_prefetch=6,
        grid=(n_tiles + 1,),
        in_specs=[any_space, any_space, any_space, any_space],
        out_specs=pl.BlockSpec((MOE_TILE, D_MODEL // 2), lambda n, *_: (n, 0)),
        scratch_shapes=[pltpu.VMEM((2, MOE_TILE, D_MODEL // 2), U32),
                        pltpu.VMEM((2, D_MODEL, D_EXPERT), F32),
                        pltpu.VMEM((2, D_MODEL, D_EXPERT), F32),
                        pltpu.VMEM((2, D_EXPERT, D_MODEL), F32),
                        pltpu.VMEM((D_MODEL, D_EXPERT), BF16),
                        pltpu.VMEM((D_MODEL, D_EXPERT), BF16),
                        pltpu.VMEM((D_EXPERT, D_MODEL), BF16),
                        pltpu.SemaphoreType.DMA((2,)),
                        pltpu.SemaphoreType.DMA((2, 3))],
    )
    return pl.pallas_call(
        _experts_body,
        grid_spec=grid_spec,
        out_shape=jax.ShapeDtypeStruct(((n_tiles + 1) * MOE_TILE, D_MODEL // 2), U32),
        compiler_params=_cparams(("arbitrary",), 56),
        name="moe_experts",
    )(tile_expert, next_expert, w_slot, tile_base, n_used, tok_sorted, hfp, w_gate, w_up, w_down)


def _combine_body(dcur_ref, dnxt_ref, h1_ref, route_ref, y_ref, o_ref, ybuf, sem, *, n_steps):
    i = pl.program_id(0)
    n = dcur_ref.shape[2]

    def row_copy(idx_ref, a, slot):
        return pltpu.make_async_copy(y_ref.at[pl.ds(idx_ref[0, 0, a], 1), :],
                                     ybuf.at[slot, a % 2, pl.ds(a // 2, 1), :], sem.at[slot])

    @pl.when(i == 0)
    def _():
        for a in range(n):
            row_copy(dcur_ref, a, 0).start()

    def row_wait(a, slot):
        pltpu.make_async_copy(y_ref.at[pl.ds(0, 1), :], ybuf.at[slot, a % 2, pl.ds(a // 2, 1), :],
                              sem.at[slot]).wait()

    def tile(slot):
        for a in range(n):
            row_copy(dnxt_ref, a, 1 - slot).start(priority=a % 2)
        for a in range(n):
            row_wait(a, slot)
        route = route_ref[...]
        g0 = route[:, 0:1]
        g1 = route[:, 1:2]
        u0 = ybuf[slot, 0]
        u1 = ybuf[slot, 1]
        half = D_MODEL // 2
        hi_mask = jnp.uint32(0xFFFF0000)
        o_ref[:, :half] = h1_ref[:, :half] + (pltpu.bitcast(u0 << 16, F32) * g0 + pltpu.bitcast(u1 << 16, F32) * g1)
        o_ref[:, half:] = h1_ref[:, half:] + (pltpu.bitcast(u0 & hi_mask, F32) * g0
                                              + pltpu.bitcast(u1 & hi_mask, F32) * g1)

    for parity in range(2):
        pl.when((i < n_steps) & ((i & 1) == parity))(functools.partial(tile, parity))

    @pl.when(i == n_steps)
    def _():
        for a in range(n):
            row_wait(a, i & 1)


def _combine(dest3, h1, route, y):
    m = h1.shape[0]
    tm = dest3.shape[2] // 2
    n_steps = m // tm
    tok = lambda i: (jnp.minimum(i, n_steps - 1), 0)
    return pl.pallas_call(
        functools.partial(_combine_body, n_steps=n_steps),
        grid=(n_steps + 1,),
        in_specs=[
            pl.BlockSpec((1, 1, 2 * tm), lambda i: (i, 0, 0), memory_space=pltpu.SMEM),
            pl.BlockSpec((1, 1, 2 * tm), lambda i: (jnp.minimum(i + 1, n_steps), 0, 0), memory_space=pltpu.SMEM),
            pl.BlockSpec((tm, D_MODEL), tok),
            pl.BlockSpec((tm, LANES), tok),
            pl.BlockSpec(memory_space=pl.ANY),
        ],
        out_specs=pl.BlockSpec((tm, D_MODEL), tok),
        out_shape=jax.ShapeDtypeStruct((m, D_MODEL), F32),
        scratch_shapes=[pltpu.VMEM((2, 2, tm, D_MODEL // 2), U32), pltpu.SemaphoreType.DMA((2,))],
        compiler_params=_cparams(("arbitrary",), 40),
        name="moe_combine",
    )(dest3, dest3, h1, route, y)


def _routing_tables(route, n_tok):
    a_expert = route[:, 2:4].astype(I32).reshape(-1)
    n_assign = a_expert.shape[0]
    eids = jnp.arange(N_EXPERTS, dtype=I32)
    sorted_e, order = lax.sort((a_expert, jnp.arange(n_assign, dtype=I32)), num_keys=1, is_stable=True)
    first = jnp.sum((sorted_e[None, :] < eids[:, None]).astype(I32), axis=1)
    counts = jnp.concatenate([first[1:], jnp.full((1,), n_assign, I32)]) - first
    padded = (counts + MOE_TILE - 1) // MOE_TILE * MOE_TILE
    upto = (eids[None, :] <= eids[:, None]).astype(I32)
    seg_end = jnp.sum(upto * padded[None, :], axis=1)
    seg_start = seg_end - padded
    shift = seg_start - first
    shift_p = jnp.sum(jnp.where(sorted_e[None, :] == eids[:, None], shift[:, None], 0), axis=0)
    _, dest = lax.sort((order, jnp.arange(n_assign, dtype=I32) + shift_p), num_keys=1)

    n_tiles = n_assign // MOE_TILE + N_EXPERTS
    n_used = (seg_end[-1] // MOE_TILE).astype(I32)
    tile_start = jnp.arange(n_tiles + 2, dtype=I32) * MOE_TILE
    te = jnp.sum((seg_end[None, :] <= tile_start[:, None]).astype(I32), axis=1)
    te = jnp.minimum(te, N_EXPERTS - 1).astype(I32)
    used = counts > 0
    later_used = used[None, :] & (eids[None, :] > eids[:, None])
    next_e = jnp.min(jnp.where(later_used, eids[None, :], N_EXPERTS), axis=1)
    next_e = jnp.where(next_e == N_EXPERTS, -1, next_e).astype(I32)
    ordinal = jnp.sum(upto * used.astype(I32)[None, :], axis=1) - 1
    base = first[te] + tile_start - seg_start[te]
    base = jnp.where(jnp.arange(n_tiles + 2) < n_used, jnp.clip(base, 0, n_assign), n_assign).astype(I32)
    tok_sorted = jnp.concatenate([order // 2, jnp.zeros((MOE_TILE,), I32)])
    return (dest.astype(I32), tok_sorted, base, te[:n_tiles], next_e[te[:n_tiles]],
            (ordinal[te[:n_tiles]] & 1).astype(I32), n_used.reshape(1))


def kernel(x, meta_tokens, rel_bias, ln_mix, w_in, q_norm, k_norm, sinks, out_norm_sb, out_norm_swa,
           w_out, ln_ffn, w_router_group, b_router_group, w_router_expert, b_router_expert,
           w_gate, w_up, w_down):
    batch, seq, _ = x.shape
    n_tok = batch * seq
    x2d = x.reshape(n_tok, D_MODEL)

    ones = lambda n: jnp.ones((n,), F32)
    col_scale = jnp.concatenate([
        ones(SB_WIDTH) * SCALE, ones(2 * SB_WIDTH),
        jnp.tile(q_norm[0], SWA_Q_HEADS) * SCALE, jnp.tile(k_norm[0], SWA_KV_HEADS), ones(SWA_KV_WIDTH)])[None]
    norm_flag = jnp.concatenate([jnp.zeros((3 * SB_WIDTH,), F32), ones(SWA_WIDTH + SWA_KV_WIDTH),
                                 jnp.zeros((SWA_KV_WIDTH,), F32)])[None]
    w_in_bf = w_in[0].astype(BF16)
    gain_mix = ln_mix[0][None]

    proj = _norm_proj(x2d, gain_mix, w_in_bf, col_scale, norm_flag)
    meta_proj = _norm_proj(meta_tokens.astype(F32), gain_mix, w_in_bf, col_scale, norm_flag)
    meta_blk = jnp.concatenate([jnp.zeros((META_PAD, IN_PROJ_WIDTH), BF16), meta_proj], axis=0)

    o_sb = _sb_attention(proj, meta_blk, batch, seq)
    o_sw = _swa_attention(proj, meta_blk, _bucket_tables(), rel_bias.astype(F32), sinks[0].astype(F32),
                          batch, seq)

    w_router = jnp.concatenate(
        [w_router_group[0], w_router_expert[0],
         jnp.zeros((D_MODEL, LANES - N_GROUPS - N_EXPERTS), F32)], axis=1).astype(BF16)
    b_router = jnp.concatenate(
        [b_router_group[0], b_router_expert[0], jnp.zeros((LANES - N_GROUPS - N_EXPERTS,), F32)])[None]
    h1, hfp, route = _out_router(o_sb, o_sw, x2d, out_norm_sb[0][None], out_norm_swa[0][None],
                                 w_out[0].astype(BF16), ln_ffn[0][None], w_router, b_router)

    dest, tok_sorted, tile_base, tile_expert, next_expert, w_slot, n_used = _routing_tables(route, n_tok)
    y = _experts(tile_expert, next_expert, w_slot, tile_base, n_used, tok_sorted, hfp,
                 w_gate[0], w_up[0], w_down[0])
    dest3 = jnp.concatenate([dest, jnp.zeros((2 * TOK_TILE,), I32)]).reshape(-1, 1, 2 * TOK_TILE)
    out = _combine(dest3, h1, route, y)
    return out.reshape(batch, seq, D_MODEL)
```

```python
import functools
import math

import numpy as np
import jax
import jax.numpy as jnp
from jax import lax
from jax.experimental import pallas as pl
from jax.experimental.pallas import tpu as pltpu

F32 = jnp.float32
BF16 = jnp.bfloat16
I32 = jnp.int32
U32 = jnp.uint32

D_MODEL = 2048
HEAD_DIM = 64
LANES = 128
SB_HEADS = 16
SWA_Q_HEADS = 16
SWA_KV_HEADS = 4
SB_WIDTH = SB_HEADS * HEAD_DIM
SWA_WIDTH = SWA_Q_HEADS * HEAD_DIM
SWA_KV_WIDTH = SWA_KV_HEADS * HEAD_DIM
IN_PROJ_WIDTH = 3 * SB_WIDTH + SWA_WIDTH + 2 * SWA_KV_WIDTH
BLOCK = 128
WINDOW = 128
N_META = 16
META_PAD = BLOCK - N_META
NUM_BUCKETS = 32
MAX_DISTANCE = 128
N_GROUPS = 8
EXPERTS_PER_GROUP = 8
N_EXPERTS = N_GROUPS * EXPERTS_PER_GROUP
D_EXPERT = D_MODEL // 4
RMS_EPS = 1e-6
SCALE = HEAD_DIM ** -0.5
LOG2E = 1.4426950408889634

COL_Q_SB = 0
COL_K_SB = SB_WIDTH // LANES
COL_V_SB = 2 * SB_WIDTH // LANES
COL_Q_SW = 3 * SB_WIDTH // LANES
COL_K_SW = COL_Q_SW + SWA_WIDTH // LANES
COL_V_SW = COL_K_SW + SWA_KV_WIDTH // LANES

PROJ_TN = 768
PROJ_TM = 512
OUT_TM = 512
SB_CHUNK = 256
MOE_TILE = 256
TOK_TILE = 256
NEG_INF = float("-inf")


def _cparams(sem, vmem_mb):
    return pltpu.CompilerParams(dimension_semantics=sem, vmem_limit_bytes=vmem_mb * 1024 * 1024)


SUBLANES = 8
ROW_WORDS = D_MODEL // 2
assert ROW_WORDS == SUBLANES * LANES


def _store_row_tiles(ref, row0, value, lead=()):
    rows = value.shape[0]
    for c in range(SUBLANES):
        ref[lead + (pl.ds(SUBLANES * row0 + c, rows, stride=SUBLANES), slice(None))] = \
            value[:, c * LANES:(c + 1) * LANES]


def _load_row_tiles(ref, rows, lead=()):
    return jnp.concatenate(
        [ref[lead + (pl.ds(c, rows, stride=SUBLANES), slice(None))] for c in range(SUBLANES)], axis=1)


def _norm_proj_body(x_ref, g_ref, w_ref, cs_ref, nf_ref, o_ref, *, n_plain_tiles):
    x = x_ref[...]
    var = jnp.mean(x * x, axis=-1, keepdims=True)
    hn = (x * lax.rsqrt(var + RMS_EPS) * g_ref[...]).astype(BF16)
    lane = lax.broadcasted_iota(I32, (1, LANES), 1)
    lo = lane < HEAD_DIM
    for j in range(w_ref.shape[1] // PROJ_TN):
        cols = slice(j * PROJ_TN, (j + 1) * PROJ_TN)
        acc = jnp.dot(hn, w_ref[:, cols], preferred_element_type=F32)
        if j < n_plain_tiles:
            o_ref[:, cols] = (acc * cs_ref[:, cols]).astype(o_ref.dtype)
            continue
        for c in range(PROJ_TN // LANES):
            sl = slice(j * PROJ_TN + c * LANES, j * PROJ_TN + (c + 1) * LANES)
            a = acc[:, c * LANES:(c + 1) * LANES]
            s = a * a
            s_lo = jnp.sum(jnp.where(lo, s, 0.0), axis=-1, keepdims=True)
            s_hi = jnp.sum(jnp.where(lo, 0.0, s), axis=-1, keepdims=True)
            ms = jnp.where(lo, s_lo, s_hi) * (1.0 / HEAD_DIM)
            r = jnp.where(nf_ref[:, sl] > 0.0, lax.rsqrt(ms + RMS_EPS), 1.0)
            o_ref[:, sl] = ((a * r) * cs_ref[:, sl]).astype(o_ref.dtype)


def _norm_proj(x2d, gain, w_bf, col_scale, norm_flag):
    m = x2d.shape[0]
    tm = min(PROJ_TM, m)
    n = w_bf.shape[1]
    n_plain = (3 * SB_WIDTH) // PROJ_TN
    whole = lambda r, c: pl.BlockSpec((r, c), lambda i: (0, 0))
    return pl.pallas_call(
        functools.partial(_norm_proj_body, n_plain_tiles=n_plain),
        grid=(m // tm,),
        in_specs=[pl.BlockSpec((tm, D_MODEL), lambda i: (i, 0)), whole(1, D_MODEL), whole(D_MODEL, n),
                  whole(1, n), whole(1, n)],
        out_specs=pl.BlockSpec((tm, n), lambda i: (i, 0)),
        out_shape=jax.ShapeDtypeStruct((m, n), BF16),
        compiler_params=_cparams(("parallel",), 56),
        name="norm_proj",
    )(x2d, gain, w_bf, col_scale, norm_flag)


SB_SKIP_LOG = 104.0


SB_QB = 8


def _sb_body(q_ref, k_ref, v_ref, km_ref, vm_ref, o_ref, kx_ref, vx_ref, acc_ref, car_ref, *, seq):
    g = pl.program_id(2)

    @pl.when(g == 0)
    def _():
        zeros = jnp.zeros((BLOCK, LANES), BF16)
        kx_ref[0:BLOCK, :] = zeros
        vx_ref[0:BLOCK, :] = zeros
        kx_ref[BLOCK:2 * BLOCK, :] = km_ref[...]
        vx_ref[BLOCK:2 * BLOCK, :] = vm_ref[...]
        kx_ref[2 * BLOCK:, :] = k_ref[...]
        vx_ref[2 * BLOCK:, :] = v_ref[...]

    lane = lax.broadcasted_iota(I32, (BLOCK, LANES), 1)
    lo = lane < HEAD_DIM

    def later_key_matrix(nk):
        r = lax.broadcasted_iota(I32, (nk, nk), 0)
        c = lax.broadcasted_iota(I32, (nk, nk), 1)
        return (r > c).astype(F32).astype(BF16)

    tri = {BLOCK: later_key_matrix(BLOCK), SB_CHUNK: later_key_matrix(SB_CHUNK)}

    def stage_qk(qm, start, nk):
        kc = kx_ref[pl.ds(start, nk), :]
        return lax.dot_general(qm, kc, (((1,), (1,)), ((), ())), preferred_element_type=F32)

    def stage_soft(s, start, mask):
        nk = s.shape[1]
        sp = jnp.maximum(s, 0.0) + jnp.log(1.0 + jnp.exp2(jnp.abs(s) * (-LOG2E)))
        log_beta = s - sp
        valid = None
        if mask is not None:
            r_i = lax.broadcasted_iota(I32, (2 * BLOCK, nk), 0)
            c_i = lax.broadcasted_iota(I32, (2 * BLOCK, nk), 1)
            if mask == "diag":
                valid = c_i < (r_i & (BLOCK - 1))
                sp = jnp.where(valid, sp, 0.0)
            else:
                sp = jnp.where((start + c_i) >= (BLOCK + META_PAD), sp, 0.0)
        return sp, log_beta, valid

    def stage_inner(soft, start):
        sp, log_beta, valid = soft
        nk = sp.shape[1]
        inner = jnp.dot(sp.astype(BF16), tri[nk], preferred_element_type=F32)
        return log_beta - inner, jnp.sum(sp, axis=1, keepdims=True), valid, vx_ref[pl.ds(start, nk), :]

    def scores(qm, start, nk, mask):
        return stage_inner(stage_soft(stage_qk(qm, start, nk), start, mask), start)

    def weights(part, car):
        log_w0, _, valid, _ = part
        nk = log_w0.shape[1]
        car_k = car if nk == LANES else jnp.concatenate([car, car], axis=1)
        w = jnp.exp(log_w0 - car_k)
        if valid is not None:
            w = jnp.where(valid, w, 0.0)
        return w.astype(BF16)

    def least_of(car):
        return jnp.min(jnp.min(car, axis=0, keepdims=True))

    idx, owns, qms = [], [], []
    for u in range(SB_QB):
        i = SB_QB * g + u
        idx.append(i)
        owns.append(pl.multiple_of(BLOCK * (i + 2), BLOCK))
        q = q_ref[u * BLOCK:(u + 1) * BLOCK, :]
        zq = jnp.zeros_like(q)
        qms.append(jnp.concatenate([jnp.where(lo, q, zq), jnp.where(lo, zq, q)], axis=0))
    chunks = []
    for u in range(SB_QB):
        chunks.append((u, owns[u], BLOCK, "diag"))
        chunks.append((u, pl.multiple_of(owns[u] - SB_CHUNK, BLOCK), SB_CHUNK, "low"))
    qk = [stage_qk(qms[u], start, nk) for u, start, nk, _ in chunks]
    soft = [stage_soft(s, start, mask) for s, (_, start, _, mask) in zip(qk, chunks)]
    parts = [stage_inner(sf, start) for sf, (_, start, _, _) in zip(soft, chunks)]
    tiles = []
    for u in range(SB_QB):
        diag, low1 = parts[2 * u], parts[2 * u + 1]
        car = jnp.broadcast_to(diag[1], (2 * BLOCK, LANES))
        acc = jnp.dot(jnp.where(diag[2], jnp.exp(diag[0]), 0.0).astype(BF16), diag[3],
                      preferred_element_type=F32)
        acc = acc + jnp.dot(weights(low1, car), low1[3], preferred_element_type=F32)
        car = car + low1[1]
        acc_ref[u] = acc
        car_ref[u] = car
        tiles.append((idx[u], owns[u], qms[u], least_of(car)))

    for u, (i, own, qm, least0) in enumerate(tiles):
        n_low = lax.shift_right_logical(i + 2, 1)

        def step(start, mask, u=u, qm=qm):
            part = scores(qm, start, SB_CHUNK, mask)
            car_ = car_ref[u]
            acc_ref[u] += jnp.dot(weights(part, car_), part[3], preferred_element_type=F32)
            new_car = car_ + part[1]
            car_ref[u] = new_car
            return least_of(new_car)

        def cond(st, n_low=n_low):
            j, least_ = st
            return (j < n_low) & (least_ <= SB_SKIP_LOG)

        def body(st, own=own, step=step):
            j, _ = st
            return j + 1, step(pl.multiple_of(own - j * SB_CHUNK, BLOCK), None)

        j_end, least = lax.while_loop(cond, body, (jnp.int32(2), least0))

        @pl.when((n_low >= 2) & (j_end == n_low) & (least <= SB_SKIP_LOG))
        def _(own=own, n_low=n_low, step=step):
            step(pl.multiple_of(own - n_low * SB_CHUNK, BLOCK), "low")

        acc = acc_ref[u]
        o_ref[u * BLOCK:(u + 1) * BLOCK, :] = jnp.where(lo, acc[0:BLOCK], acc[BLOCK:2 * BLOCK])


def _sb_attention(proj, meta_blk, batch, seq):
    nq = seq // BLOCK
    ng = nq // SB_QB
    n_pairs = SB_WIDTH // LANES
    rows = SB_QB * BLOCK
    return pl.pallas_call(
        functools.partial(_sb_body, seq=seq),
        grid=(batch, n_pairs, ng),
        in_specs=[
            pl.BlockSpec((rows, LANES), lambda b, h, g: (b * ng + g, COL_Q_SB + h)),
            pl.BlockSpec((seq, LANES), lambda b, h, g: (b, COL_K_SB + h)),
            pl.BlockSpec((seq, LANES), lambda b, h, g: (b, COL_V_SB + h)),
            pl.BlockSpec((BLOCK, LANES), lambda b, h, g: (0, COL_K_SB + h)),
            pl.BlockSpec((BLOCK, LANES), lambda b, h, g: (0, COL_V_SB + h)),
        ],
        out_specs=pl.BlockSpec((rows, LANES), lambda b, h, g: (b * ng + g, h)),
        out_shape=jax.ShapeDtypeStruct((batch * seq, SB_WIDTH), F32),
        scratch_shapes=[
            pltpu.VMEM((seq + 2 * BLOCK, LANES), BF16),
            pltpu.VMEM((seq + 2 * BLOCK, LANES), BF16),
            pltpu.VMEM((SB_QB, 2 * BLOCK, LANES), F32),
            pltpu.VMEM((SB_QB, 2 * BLOCK, LANES), F32),
        ],
        compiler_params=_cparams(("parallel", "parallel", "arbitrary"), 32),
        name="sb_attn",
    )(proj, proj, proj, meta_blk, meta_blk)


SWA_QB = 8


def _swa_body(q_ref, kp_ref, kc_ref, vp_ref, vc_ref, km_ref, vm_ref, bkt_ref,
              rel_ref, sink_ref, o_ref, bias_ref):
    t = pl.program_id(0)
    g = pl.program_id(2)
    h0 = 2 * t
    kv_in_hi = (lax.shift_right_logical(t, 1) & 1) == 1

    @pl.when((pl.program_id(1) == 0) & (g == 0))
    def _():
        for tb in range(2):
            bkt = bkt_ref[tb]
            for hh in range(2):
                bias = jnp.full(bkt.shape, NEG_INF, F32)
                for b in range(NUM_BUCKETS):
                    bias = jnp.where(bkt == b, rel_ref[b, h0 + hh], bias)
                bias_ref[tb, hh * BLOCK:(hh + 1) * BLOCK, :] = bias

    lane = lax.broadcasted_iota(I32, (BLOCK, LANES), 1)
    lo = lane < HEAD_DIM
    kmask = jnp.logical_xor(lo, kv_in_hi)
    rcol = lax.broadcasted_iota(I32, (2 * BLOCK, 1), 0)
    sink = jnp.where(rcol < BLOCK, sink_ref[h0], sink_ref[h0 + 1])
    k_meta = km_ref[...]
    v_meta = vm_ref[...]

    blocks = range(SWA_QB)
    rows = [slice(u * BLOCK, (u + 1) * BLOCK) for u in blocks]
    scores, values = [], []
    for u in blocks:
        q = q_ref[rows[u], :]
        q_sw = jnp.concatenate([q[:, HEAD_DIM:], q[:, :HEAD_DIM]], axis=1)
        zq = jnp.zeros_like(q)
        qa = jnp.where(kmask, jnp.where(kv_in_hi, q_sw, q), zq)
        qb = jnp.where(kmask, jnp.where(kv_in_hi, q, q_sw), zq)
        qm = jnp.concatenate([qa, qb], axis=0)
        k_prev = kp_ref[...] if u == 0 else kc_ref[rows[u - 1], :]
        v_prev = vp_ref[...] if u == 0 else vc_ref[rows[u - 1], :]
        k_all = jnp.concatenate([k_prev, kc_ref[rows[u], :], k_meta], axis=0)
        values.append(jnp.concatenate([v_prev, vc_ref[rows[u], :], v_meta], axis=0))
        bias = bias_ref[jnp.minimum(g, 1)] if u == 0 else bias_ref[1]
        scores.append(lax.dot_general(qm, k_all, (((1,), (1,)), ((), ())), preferred_element_type=F32) + bias)
    peaks = [jnp.maximum(jnp.max(s, axis=-1, keepdims=True), sink) for s in scores]
    probs = [jnp.exp(s - m) for s, m in zip(scores, peaks)]
    denoms = [jnp.sum(p, axis=-1, keepdims=True) + jnp.exp(sink - m) for p, m in zip(probs, peaks)]
    outs = [jnp.dot((p * (1.0 / d)).astype(BF16), v, preferred_element_type=F32)
            for p, d, v in zip(probs, denoms, values)]
    for u in blocks:
        oa = outs[u][0:BLOCK]
        ob = outs[u][BLOCK:2 * BLOCK]
        oa_r = pltpu.roll(oa, HEAD_DIM, 1)
        ob_r = pltpu.roll(ob, HEAD_DIM, 1)
        o_ref[rows[u], :] = jnp.where(lo, jnp.where(kv_in_hi, oa_r, oa), jnp.where(kv_in_hi, ob, ob_r))


def _swa_attention(proj, meta_blk, bkt, rel_bias, sinks, batch, seq):
    nq = seq // BLOCK
    ng = nq // SWA_QB
    n_pairs = SWA_WIDTH // LANES
    rows = SWA_QB * BLOCK

    def cur(base):
        return lambda t, b, g: (b * ng + g, base + t // 4)

    def prev(base):
        return lambda t, b, g: (b * nq + jnp.maximum(SWA_QB * g - 1, 0), base + t // 4)

    smem = pl.BlockSpec(memory_space=pltpu.SMEM)
    return pl.pallas_call(
        _swa_body,
        grid=(n_pairs, batch, ng),
        in_specs=[
            pl.BlockSpec((rows, LANES), lambda t, b, g: (b * ng + g, COL_Q_SW + t)),
            pl.BlockSpec((BLOCK, LANES), prev(COL_K_SW)),
            pl.BlockSpec((rows, LANES), cur(COL_K_SW)),
            pl.BlockSpec((BLOCK, LANES), prev(COL_V_SW)),
            pl.BlockSpec((rows, LANES), cur(COL_V_SW)),
            pl.BlockSpec((BLOCK, LANES), lambda t, b, g: (0, COL_K_SW + t // 4)),
            pl.BlockSpec((BLOCK, LANES), lambda t, b, g: (0, COL_V_SW + t // 4)),
            pl.BlockSpec((2, BLOCK, 3 * BLOCK), lambda t, b, g: (0, 0, 0)),
            smem,
            smem,
        ],
        out_specs=pl.BlockSpec((rows, LANES), lambda t, b, g: (b * ng + g, t)),
        out_shape=jax.ShapeDtypeStruct((batch * seq, SWA_WIDTH), F32),
        scratch_shapes=[pltpu.VMEM((2, 2 * BLOCK, 3 * BLOCK), F32)],
        compiler_params=_cparams(("arbitrary", "arbitrary", "arbitrary"), 32),
        name="swa_attn",
    )(proj, proj, proj, proj, proj, meta_blk, meta_blk, bkt, rel_bias, sinks)


def _t5_bucket_np(dist):
    max_exact = NUM_BUCKETS // 2
    d = np.maximum(dist, 0)
    ratio = np.maximum(d, 1).astype(np.float32) / np.float32(max_exact)
    large = max_exact + (np.log(ratio) / np.float32(math.log(MAX_DISTANCE / max_exact))
                         * np.float32(NUM_BUCKETS - max_exact)).astype(np.int32)
    large = np.minimum(large, NUM_BUCKETS - 1)
    return np.where(d < max_exact, d, large).astype(np.int32)


def _bucket_tables():
    ql = np.arange(BLOCK)[:, None]
    sl = np.arange(2 * BLOCK)[None, :]
    dist = ql + BLOCK - sl
    ok = (dist >= 0) & (dist < WINDOW)
    band = np.where(ok, _t5_bucket_np(dist), -1).astype(np.int32)
    band_first = np.where(sl >= BLOCK, band, -1).astype(np.int32)
    ms = np.arange(BLOCK)[None, :]
    tables = []
    for blk, b in ((0, band_first), (1, band)):
        dist_m = ql + (blk + 1) * BLOCK - ms
        meta = np.where(ms >= META_PAD, _t5_bucket_np(dist_m), -1).astype(np.int32)
        tables.append(np.concatenate([b, meta], axis=1))
    return jnp.asarray(np.stack(tables))


def _out_router_body(osb_ref, osw_ref, x_ref, gsb_ref, gsw_ref, wo_ref, lf_ref, wr_ref, br_ref,
                     h1_ref, hfp_ref, route_ref):
    tm = x_ref.shape[0]
    n_groups = max(1, tm // TOK_TILE)
    rows = [slice(r * (tm // n_groups), (r + 1) * (tm // n_groups)) for r in range(n_groups)]

    def branch_norm(o_ref_, g_ref_, rs):
        a = o_ref_[rs, :]
        var = jnp.mean(a * a, axis=-1, keepdims=True)
        return ((a * lax.rsqrt(var + RMS_EPS)) * g_ref_[...]).astype(BF16)

    mixed = [jnp.concatenate([branch_norm(osb_ref, gsb_ref, rs), branch_norm(osw_ref, gsw_ref, rs)], axis=1)
             for rs in rows]
    h1s = [x_ref[rs, :] + jnp.dot(mx, wo_ref[...], preferred_element_type=F32) for mx, rs in zip(mixed, rows)]
    hfs = []
    for h1, rs in zip(h1s, rows):
        h1_ref[rs, :] = h1
        var = jnp.mean(h1 * h1, axis=-1, keepdims=True)
        hf = ((h1 * lax.rsqrt(var + RMS_EPS)) * lf_ref[...]).astype(BF16)
        hfs.append(hf)
        half = D_MODEL // 2
        lo_bits = pltpu.bitcast(hf[:, :half].astype(F32), U32)
        hi_bits = pltpu.bitcast(hf[:, half:].astype(F32), U32)
        _store_row_tiles(hfp_ref, rs.start, (hi_bits & jnp.uint32(0xFFFF0000)) | (lo_bits >> 16))
    all_logits = [jnp.dot(hf, wr_ref[...], preferred_element_type=F32) + br_ref[...] for hf in hfs]

    for logits, rs in zip(all_logits, rows):
        lane = lax.broadcasted_iota(I32, logits.shape, 1)
        big = jnp.int32(LANES)

        def softmax_masked(mask, logits=logits):
            z = jnp.where(mask, logits, NEG_INF)
            zmax = jnp.max(z, axis=-1, keepdims=True)
            e = jnp.exp(z - zmax)
            return e / jnp.sum(e, axis=-1, keepdims=True)

        def top1(p, mask, lane=lane):
            pm = jnp.where(mask, p, -1.0)
            v = jnp.max(pm, axis=-1, keepdims=True)
            idx = jnp.min(jnp.where(pm == v, lane, big), axis=-1, keepdims=True)
            return v, idx

        gmask = lane < N_GROUPS
        g_val, g_idx = top1(softmax_masked(gmask), gmask)
        e_lo = N_GROUPS + EXPERTS_PER_GROUP * g_idx
        emask = (lane >= e_lo) & (lane < e_lo + EXPERTS_PER_GROUP)
        e_prob = softmax_masked(emask)
        v1, i1 = top1(e_prob, emask)
        emask2 = emask & (lane != i1)
        v2, i2 = top1(e_prob, emask2)
        norm = v1 + v2
        gate1 = g_val * v1 / norm
        gate2 = g_val * v2 / norm
        ex1 = (i1 - N_GROUPS).astype(F32)
        ex2 = (i2 - N_GROUPS).astype(F32)
        route_ref[rs, :] = jnp.where(lane == 0, gate1,
                                     jnp.where(lane == 1, gate2,
                                               jnp.where(lane == 2, ex1,
                                                         jnp.where(lane == 3, ex2, 0.0))))


def _out_router(o_sb, o_sw, x2d, g_sb, g_sw, wo_bf, ln_ffn, w_router, b_router):
    m = x2d.shape[0]
    tm = min(OUT_TM, m)
    row = lambda w: pl.BlockSpec((tm, w), lambda i: (i, 0))
    const = lambda r, c: pl.BlockSpec((r, c), lambda i: (0, 0))
    return pl.pallas_call(
        _out_router_body,
        grid=(m // tm,),
        in_specs=[row(SB_WIDTH), row(SWA_WIDTH), row(D_MODEL), const(1, SB_WIDTH), const(1, SWA_WIDTH),
                  const(D_MODEL, D_MODEL), const(1, D_MODEL), const(D_MODEL, LANES), const(1, LANES)],
        out_specs=[row(D_MODEL), pl.BlockSpec((SUBLANES * tm, LANES), lambda i: (i, 0)), row(LANES)],
        out_shape=[jax.ShapeDtypeStruct((m, D_MODEL), F32),
                   jax.ShapeDtypeStruct((SUBLANES * m, LANES), U32),
                   jax.ShapeDtypeStruct((m, LANES), F32)],
        compiler_params=_cparams(("parallel",), 56),
        name="out_router",
    )(o_sb, o_sw, x2d, g_sb, g_sw, wo_bf, ln_ffn, w_router, b_router)


W_DMA_PRIORITY = 1


def _experts_body(te_ref, nx_ref, ws_ref, base_ref, nu_ref, tok_ref, hfp_ref, wg_ref, wu_ref, wd_ref, y_ref,
                  xbuf, wg_f, wu_f, wd_f, wg_bf, wu_bf, wd_bf, gsem, wsem):
    n = pl.program_id(0)
    n_used = nu_ref[0]

    def row_copy(tile, r, slot):
        tok = tok_ref[base_ref[tile] + r]
        return pltpu.make_async_copy(hfp_ref.at[pl.ds(pl.multiple_of(SUBLANES * tok, SUBLANES), SUBLANES), :],
                                     xbuf.at[slot, pl.ds(SUBLANES * r, SUBLANES), :], gsem.at[slot])

    def row_wait(r, slot):
        pltpu.make_async_copy(hfp_ref.at[pl.ds(0, SUBLANES), :], xbuf.at[slot, pl.ds(SUBLANES * r, SUBLANES), :],
                              gsem.at[slot]).wait()

    def weight_copies(e, slot):
        return (pltpu.make_async_copy(wg_ref.at[e], wg_f.at[slot], wsem.at[slot, 0]),
                pltpu.make_async_copy(wu_ref.at[e], wu_f.at[slot], wsem.at[slot, 1]),
                pltpu.make_async_copy(wd_ref.at[e], wd_f.at[slot], wsem.at[slot, 2]))

    @pl.when(n == 0)
    def _():
        for c in weight_copies(te_ref[0], 0):
            c.start(priority=W_DMA_PRIORITY)
        for r in range(MOE_TILE):
            row_copy(0, r, 0).start()

    @pl.when(n < n_used)
    def _():
        prev = te_ref[jnp.maximum(n - 1, 0)]
        fresh = (n == 0) | (te_ref[n] != prev)

        @pl.when(fresh)
        def _():
            ws = ws_ref[n]
            for c in weight_copies(te_ref[n], ws):
                c.wait()

            @pl.when(nx_ref[n] >= 0)
            def _():
                for c in weight_copies(nx_ref[n], 1 - ws):
                    c.start(priority=W_DMA_PRIORITY)

            wg_bf[...] = wg_f[ws].astype(BF16)
            wu_bf[...] = wu_f[ws].astype(BF16)
            wd_bf[...] = wd_f[ws].astype(BF16)

        def tile(slot):
            for r in range(MOE_TILE):
                row_copy(n + 1, r, 1 - slot).start()
            for r in range(MOE_TILE):
                row_wait(r, slot)

            u = _load_row_tiles(xbuf, MOE_TILE, (slot,))
            x_lo = pltpu.bitcast(u << 16, F32).astype(BF16)
            x_hi = pltpu.bitcast(u & jnp.uint32(0xFFFF0000), F32).astype(BF16)
            x = jnp.concatenate([x_lo, x_hi], axis=1)
            g = jnp.dot(x, wg_bf[...], preferred_element_type=F32)
            up = jnp.dot(x, wu_bf[...], preferred_element_type=F32)
            hdn = (g * jax.nn.sigmoid(g) * up).astype(BF16)
            y = jnp.dot(hdn, wd_bf[...], preferred_element_type=F32).astype(BF16)
            half = D_MODEL // 2
            y_lo = pltpu.bitcast(y[:, :half].astype(F32), U32)
            y_hi = pltpu.bitcast(y[:, half:].astype(F32), U32)
            _store_row_tiles(y_ref, 0, (y_hi & jnp.uint32(0xFFFF0000)) | (y_lo >> 16))

        for parity in range(2):
            pl.when((n & 1) == parity)(functools.partial(tile, parity))

    @pl.when(n == n_used)
    def _():
        for r in range(MOE_TILE):
            row_wait(r, n & 1)

    @pl.when(n >= n_used)
    def _():
        y_ref[...] = jnp.zeros_like(y_ref)


def _experts(tile_expert, next_expert, w_slot, tile_base, n_used, tok_sorted, hfp, w_gate, w_up, w_down):
    n_tiles = tile_expert.shape[0]
    any_space = pl.BlockSpec(memory_space=pl.ANY)
    grid_spec = pltpu.PrefetchScalarGridSpec(
        num_scalar_prefetch=6,
        grid=(n_tiles + 1,),
        in_specs=[any_space, any_space, any_space, any_space],
        out_specs=pl.BlockSpec((SUBLANES * MOE_TILE, LANES), lambda n, *_: (n, 0)),
        scratch_shapes=[pltpu.VMEM((2, SUBLANES * MOE_TILE, LANES), U32),
                        pltpu.VMEM((2, D_MODEL, D_EXPERT), F32),
                        pltpu.VMEM((2, D_MODEL, D_EXPERT), F32),
                        pltpu.VMEM((2, D_EXPERT, D_MODEL), F32),
                        pltpu.VMEM((D_MODEL, D_EXPERT), BF16),
                        pltpu.VMEM((D_MODEL, D_EXPERT), BF16),
                        pltpu.VMEM((D_EXPERT, D_MODEL), BF16),
                        pltpu.SemaphoreType.DMA((2,)),
                        pltpu.SemaphoreType.DMA((2, 3))],
    )
    return pl.pallas_call(
        _experts_body,
        grid_spec=grid_spec,
        out_shape=jax.ShapeDtypeStruct(((n_tiles + 1) * MOE_TILE * SUBLANES, LANES), U32),
        compiler_params=_cparams(("arbitrary",), 56),
        name="moe_experts",
    )(tile_expert, next_expert, w_slot, tile_base, n_used, tok_sorted, hfp, w_gate, w_up, w_down)


def _combine_body(dcur_ref, dnxt_ref, h1_ref, route_ref, y_ref, o_ref, ybuf, sem, *, n_steps):
    i = pl.program_id(0)
    n = dcur_ref.shape[2]

    def row_copy(idx_ref, a, slot):
        src_row = pl.multiple_of(SUBLANES * idx_ref[0, 0, a], SUBLANES)
        return pltpu.make_async_copy(y_ref.at[pl.ds(src_row, SUBLANES), :],
                                     ybuf.at[slot, a % 2, pl.ds(SUBLANES * (a // 2), SUBLANES), :], sem.at[slot])

    @pl.when(i == 0)
    def _():
        for a in range(n):
            row_copy(dcur_ref, a, 0).start()

    def row_wait(a, slot):
        pltpu.make_async_copy(y_ref.at[pl.ds(0, SUBLANES), :],
                              ybuf.at[slot, a % 2, pl.ds(SUBLANES * (a // 2), SUBLANES), :], sem.at[slot]).wait()

    def tile(slot):
        for a in range(n):
            row_copy(dnxt_ref, a, 1 - slot).start(priority=a % 2)
        for a in range(n):
            row_wait(a, slot)
        route = route_ref[...]
        g0 = route[:, 0:1]
        g1 = route[:, 1:2]
        u0 = _load_row_tiles(ybuf, n // 2, (slot, 0))
        u1 = _load_row_tiles(ybuf, n // 2, (slot, 1))
        half = D_MODEL // 2
        hi_mask = jnp.uint32(0xFFFF0000)
        o_ref[:, :half] = h1_ref[:, :half] + (pltpu.bitcast(u0 << 16, F32) * g0 + pltpu.bitcast(u1 << 16, F32) * g1)
        o_ref[:, half:] = h1_ref[:, half:] + (pltpu.bitcast(u0 & hi_mask, F32) * g0
                                              + pltpu.bitcast(u1 & hi_mask, F32) * g1)

    for parity in range(2):
        pl.when((i < n_steps) & ((i & 1) == parity))(functools.partial(tile, parity))

    @pl.when(i == n_steps)
    def _():
        for a in range(n):
            row_wait(a, i & 1)


def _combine(dest3, h1, route, y):
    m = h1.shape[0]
    tm = dest3.shape[2] // 2
    n_steps = m // tm
    tok = lambda i: (jnp.minimum(i, n_steps - 1), 0)
    return pl.pallas_call(
        functools.partial(_combine_body, n_steps=n_steps),
        grid=(n_steps + 1,),
        in_specs=[
            pl.BlockSpec((1, 1, 2 * tm), lambda i: (i, 0, 0), memory_space=pltpu.SMEM),
            pl.BlockSpec((1, 1, 2 * tm), lambda i: (jnp.minimum(i + 1, n_steps), 0, 0), memory_space=pltpu.SMEM),
            pl.BlockSpec((tm, D_MODEL), tok),
            pl.BlockSpec((tm, LANES), tok),
            pl.BlockSpec(memory_space=pl.ANY),
        ],
        out_specs=pl.BlockSpec((tm, D_MODEL), tok),
        out_shape=jax.ShapeDtypeStruct((m, D_MODEL), F32),
        scratch_shapes=[pltpu.VMEM((2, 2, SUBLANES * tm, LANES), U32), pltpu.SemaphoreType.DMA((2,))],
        compiler_params=_cparams(("arbitrary",), 40),
        name="moe_combine",
    )(dest3, dest3, h1, route, y)


def _routing_tables(route, n_tok):
    a_expert = route[:, 2:4].astype(I32).reshape(-1)
    n_assign = a_expert.shape[0]
    eids = jnp.arange(N_EXPERTS, dtype=I32)
    sorted_e, order = lax.sort((a_expert, jnp.arange(n_assign, dtype=I32)), num_keys=1, is_stable=True)
    first = jnp.sum((sorted_e[None, :] < eids[:, None]).astype(I32), axis=1)
    counts = jnp.concatenate([first[1:], jnp.full((1,), n_assign, I32)]) - first
    padded = (counts + MOE_TILE - 1) // MOE_TILE * MOE_TILE
    upto = (eids[None, :] <= eids[:, None]).astype(I32)
    seg_end = jnp.sum(upto * padded[None, :], axis=1)
    seg_start = seg_end - padded
    shift = seg_start - first
    shift_p = jnp.sum(jnp.where(sorted_e[None, :] == eids[:, None], shift[:, None], 0), axis=0)
    _, dest = lax.sort((order, jnp.arange(n_assign, dtype=I32) + shift_p), num_keys=1)

    n_tiles = n_assign // MOE_TILE + N_EXPERTS
    n_used = (seg_end[-1] // MOE_TILE).astype(I32)
    tile_start = jnp.arange(n_tiles + 2, dtype=I32) * MOE_TILE
    te = jnp.sum((seg_end[None, :] <= tile_start[:, None]).astype(I32), axis=1)
    te = jnp.minimum(te, N_EXPERTS - 1).astype(I32)
    used = counts > 0
    later_used = used[None, :] & (eids[None, :] > eids[:, None])
    next_e = jnp.min(jnp.where(later_used, eids[None, :], N_EXPERTS), axis=1)
    next_e = jnp.where(next_e == N_EXPERTS, -1, next_e).astype(I32)
    ordinal = jnp.sum(upto * used.astype(I32)[None, :], axis=1) - 1
    base = first[te] + tile_start - seg_start[te]
    base = jnp.where(jnp.arange(n_tiles + 2) < n_used, jnp.clip(base, 0, n_assign), n_assign).astype(I32)
    tok_sorted = jnp.concatenate([order // 2, jnp.zeros((MOE_TILE,), I32)])
    return (dest.astype(I32), tok_sorted, base, te[:n_tiles], next_e[te[:n_tiles]],
            (ordinal[te[:n_tiles]] & 1).astype(I32), n_used.reshape(1))


def kernel(x, meta_tokens, rel_bias, ln_mix, w_in, q_norm, k_norm, sinks, out_norm_sb, out_norm_swa,
           w_out, ln_ffn, w_router_group, b_router_group, w_router_expert, b_router_expert,
           w_gate, w_up, w_down):
    batch, seq, _ = x.shape
    n_tok = batch * seq
    x2d = x.reshape(n_tok, D_MODEL)

    ones = lambda n: jnp.ones((n,), F32)
    col_scale = jnp.concatenate([
        ones(SB_WIDTH) * SCALE, ones(2 * SB_WIDTH),
        jnp.tile(q_norm[0], SWA_Q_HEADS) * SCALE, jnp.tile(k_norm[0], SWA_KV_HEADS), ones(SWA_KV_WIDTH)])[None]
    norm_flag = jnp.concatenate([jnp.zeros((3 * SB_WIDTH,), F32), ones(SWA_WIDTH + SWA_KV_WIDTH),
                                 jnp.zeros((SWA_KV_WIDTH,), F32)])[None]
    w_in_bf = w_in[0].astype(BF16)
    gain_mix = ln_mix[0][None]

    proj = _norm_proj(x2d, gain_mix, w_in_bf, col_scale, norm_flag)
    meta_proj = _norm_proj(meta_tokens.astype(F32), gain_mix, w_in_bf, col_scale, norm_flag)
    meta_blk = jnp.concatenate([jnp.zeros((META_PAD, IN_PROJ_WIDTH), BF16), meta_proj], axis=0)

    o_sb = _sb_attention(proj, meta_blk, batch, seq)
    o_sw = _swa_attention(proj, meta_blk, _bucket_tables(), rel_bias.astype(F32), sinks[0].astype(F32),
                          batch, seq)

    w_router = jnp.concatenate(
        [w_router_group[0], w_router_expert[0],
         jnp.zeros((D_MODEL, LANES - N_GROUPS - N_EXPERTS), F32)], axis=1).astype(BF16)
    b_router = jnp.concatenate(
        [b_router_group[0], b_router_expert[0], jnp.zeros((LANES - N_GROUPS - N_EXPERTS,), F32)])[None]
    h1, hfp, route = _out_router(o_sb, o_sw, x2d, out_norm_sb[0][None], out_norm_swa[0][None],
                                 w_out[0].astype(BF16), ln_ffn[0][None], w_router, b_router)

    dest, tok_sorted, tile_base, tile_expert, next_expert, w_slot, n_used = _routing_tables(route, n_tok)
    y = _experts(tile_expert, next_expert, w_slot, tile_base, n_used, tok_sorted, hfp,
                 w_gate[0], w_up[0], w_down[0])
    dest3 = jnp.concatenate([dest, jnp.zeros((2 * TOK_TILE,), I32)]).reshape(-1, 1, 2 * TOK_TILE)
    out = _combine(dest3, h1, route, y)
    return out.reshape(batch, seq, D_MODEL)
```

```python
import functools
import math

import numpy as np
import jax
import jax.numpy as jnp
from jax import lax
from jax.experimental import pallas as pl
from jax.experimental.pallas import tpu as pltpu

F32 = jnp.float32
BF16 = jnp.bfloat16
I32 = jnp.int32
U32 = jnp.uint32

D_MODEL = 2048
HEAD_DIM = 64
LANES = 128
SB_HEADS = 16
SWA_Q_HEADS = 16
SWA_KV_HEADS = 4
SB_WIDTH = SB_HEADS * HEAD_DIM
SWA_WIDTH = SWA_Q_HEADS * HEAD_DIM
SWA_KV_WIDTH = SWA_KV_HEADS * HEAD_DIM
IN_PROJ_WIDTH = 3 * SB_WIDTH + SWA_WIDTH + 2 * SWA_KV_WIDTH
BLOCK = 128
WINDOW = 128
N_META = 16
META_PAD = BLOCK - N_META
NUM_BUCKETS = 32
MAX_DISTANCE = 128
N_GROUPS = 8
EXPERTS_PER_GROUP = 8
N_EXPERTS = N_GROUPS * EXPERTS_PER_GROUP
D_EXPERT = D_MODEL // 4
RMS_EPS = 1e-6
SCALE = HEAD_DIM ** -0.5
LOG2E = 1.4426950408889634

COL_Q_SB = 0
COL_K_SB = SB_WIDTH // LANES
COL_V_SB = 2 * SB_WIDTH // LANES
COL_Q_SW = 3 * SB_WIDTH // LANES
COL_K_SW = COL_Q_SW + SWA_WIDTH // LANES
COL_V_SW = COL_K_SW + SWA_KV_WIDTH // LANES

PROJ_TN = 768
PROJ_TM = 512
OUT_TM = 512
SB_CHUNK = 256
MOE_TILE = 256
TOK_TILE = 256
NEG_INF = float("-inf")


def _cparams(sem, vmem_mb):
    return pltpu.CompilerParams(dimension_semantics=sem, vmem_limit_bytes=vmem_mb * 1024 * 1024)


SUBLANES = 8
ROW_WORDS = D_MODEL // 2
assert ROW_WORDS == SUBLANES * LANES


def _store_row_tiles(ref, row0, value, lead=()):
    rows = value.shape[0]
    for c in range(SUBLANES):
        ref[lead + (pl.ds(SUBLANES * row0 + c, rows, stride=SUBLANES), slice(None))] = \
            value[:, c * LANES:(c + 1) * LANES]


def _load_row_tiles(ref, rows, lead=()):
    return jnp.concatenate(
        [ref[lead + (pl.ds(c, rows, stride=SUBLANES), slice(None))] for c in range(SUBLANES)], axis=1)


def _norm_proj_body(x_ref, g_ref, w_ref, cs_ref, nf_ref, o_ref, *, n_plain_tiles):
    x = x_ref[...]
    var = jnp.mean(x * x, axis=-1, keepdims=True)
    hn = (x * lax.rsqrt(var + RMS_EPS) * g_ref[...]).astype(BF16)
    lane = lax.broadcasted_iota(I32, (1, LANES), 1)
    lo = lane < HEAD_DIM
    for j in range(w_ref.shape[1] // PROJ_TN):
        cols = slice(j * PROJ_TN, (j + 1) * PROJ_TN)
        acc = jnp.dot(hn, w_ref[:, cols], preferred_element_type=F32)
        if j < n_plain_tiles:
            o_ref[:, cols] = (acc * cs_ref[:, cols]).astype(o_ref.dtype)
            continue
        for c in range(PROJ_TN // LANES):
            sl = slice(j * PROJ_TN + c * LANES, j * PROJ_TN + (c + 1) * LANES)
            a = acc[:, c * LANES:(c + 1) * LANES]
            s = a * a
            s_lo = jnp.sum(jnp.where(lo, s, 0.0), axis=-1, keepdims=True)
            s_hi = jnp.sum(jnp.where(lo, 0.0, s), axis=-1, keepdims=True)
            ms = jnp.where(lo, s_lo, s_hi) * (1.0 / HEAD_DIM)
            r = jnp.where(nf_ref[:, sl] > 0.0, lax.rsqrt(ms + RMS_EPS), 1.0)
            o_ref[:, sl] = ((a * r) * cs_ref[:, sl]).astype(o_ref.dtype)


def _norm_proj(x2d, gain, w_bf, col_scale, norm_flag):
    m = x2d.shape[0]
    tm = min(PROJ_TM, m)
    n = w_bf.shape[1]
    n_plain = (3 * SB_WIDTH) // PROJ_TN
    whole = lambda r, c: pl.BlockSpec((r, c), lambda i: (0, 0))
    return pl.pallas_call(
        functools.partial(_norm_proj_body, n_plain_tiles=n_plain),
        grid=(m // tm,),
        in_specs=[pl.BlockSpec((tm, D_MODEL), lambda i: (i, 0)), whole(1, D_MODEL), whole(D_MODEL, n),
                  whole(1, n), whole(1, n)],
        out_specs=pl.BlockSpec((tm, n), lambda i: (i, 0)),
        out_shape=jax.ShapeDtypeStruct((m, n), BF16),
        compiler_params=_cparams(("parallel",), 56),
        name="norm_proj",
    )(x2d, gain, w_bf, col_scale, norm_flag)


SB_SKIP_LOG = 104.0


SB_QB = 8
SB_STAGE_GROUP = 4


def _sb_body(q_ref, k_ref, v_ref, km_ref, vm_ref, o_ref, kx_ref, vx_ref, acc_ref, car_ref, *, seq):
    g = pl.program_id(2)

    @pl.when(g == 0)
    def _():
        zeros = jnp.zeros((BLOCK, LANES), BF16)
        kx_ref[0:BLOCK, :] = zeros
        vx_ref[0:BLOCK, :] = zeros
        kx_ref[BLOCK:2 * BLOCK, :] = km_ref[...]
        vx_ref[BLOCK:2 * BLOCK, :] = vm_ref[...]
        kx_ref[2 * BLOCK:, :] = k_ref[...]
        vx_ref[2 * BLOCK:, :] = v_ref[...]

    lane = lax.broadcasted_iota(I32, (BLOCK, LANES), 1)
    lo = lane < HEAD_DIM

    def later_key_matrix(nk):
        r = lax.broadcasted_iota(I32, (nk, nk), 0)
        c = lax.broadcasted_iota(I32, (nk, nk), 1)
        return (r > c).astype(F32).astype(BF16)

    tri = {BLOCK: later_key_matrix(BLOCK), SB_CHUNK: later_key_matrix(SB_CHUNK)}

    def stage_qk(qm, start, nk):
        kc = kx_ref[pl.ds(start, nk), :]
        return lax.dot_general(qm, kc, (((1,), (1,)), ((), ())), preferred_element_type=F32)

    def stage_soft(s, start, mask):
        nk = s.shape[1]
        sp = jnp.maximum(s, 0.0) + jnp.log(1.0 + jnp.exp2(jnp.abs(s) * (-LOG2E)))
        log_beta = s - sp
        valid = None
        if mask is not None:
            r_i = lax.broadcasted_iota(I32, (2 * BLOCK, nk), 0)
            c_i = lax.broadcasted_iota(I32, (2 * BLOCK, nk), 1)
            if mask == "diag":
                valid = c_i < (r_i & (BLOCK - 1))
                sp = jnp.where(valid, sp, 0.0)
            else:
                sp = jnp.where((start + c_i) >= (BLOCK + META_PAD), sp, 0.0)
        return sp, log_beta, valid

    def stage_inner(soft, start):
        sp, log_beta, valid = soft
        nk = sp.shape[1]
        inner = jnp.dot(sp.astype(BF16), tri[nk], preferred_element_type=F32)
        return log_beta - inner, jnp.sum(sp, axis=1, keepdims=True), valid, vx_ref[pl.ds(start, nk), :]

    def scores(qm, start, nk, mask):
        return stage_inner(stage_soft(stage_qk(qm, start, nk), start, mask), start)

    def weights(part, car):
        log_w0, _, valid, _ = part
        nk = log_w0.shape[1]
        car_k = car if nk == LANES else jnp.concatenate([car, car], axis=1)
        w = jnp.exp(log_w0 - car_k)
        if valid is not None:
            w = jnp.where(valid, w, 0.0)
        return w.astype(BF16)

    def least_of(car):
        return jnp.min(jnp.min(car, axis=0, keepdims=True))

    idx, owns, qms = [], [], []
    for u in range(SB_QB):
        i = SB_QB * g + u
        idx.append(i)
        owns.append(pl.multiple_of(BLOCK * (i + 2), BLOCK))
        q = q_ref[u * BLOCK:(u + 1) * BLOCK, :]
        zq = jnp.zeros_like(q)
        qms.append(jnp.concatenate([jnp.where(lo, q, zq), jnp.where(lo, zq, q)], axis=0))
    tiles = []
    for u0 in range(0, SB_QB, SB_STAGE_GROUP):
        group = range(u0, min(u0 + SB_STAGE_GROUP, SB_QB))
        chunks = []
        for u in group:
            chunks.append((u, owns[u], BLOCK, "diag"))
            chunks.append((u, pl.multiple_of(owns[u] - SB_CHUNK, BLOCK), SB_CHUNK, "low"))
        qk = [stage_qk(qms[u], start, nk) for u, start, nk, _ in chunks]
        soft = [stage_soft(s, start, mask) for s, (_, start, _, mask) in zip(qk, chunks)]
        parts = [stage_inner(sf, start) for sf, (_, start, _, _) in zip(soft, chunks)]
        for n, u in enumerate(group):
            diag, low1 = parts[2 * n], parts[2 * n + 1]
            car = jnp.broadcast_to(diag[1], (2 * BLOCK, LANES))
            acc = jnp.dot(jnp.where(diag[2], jnp.exp(diag[0]), 0.0).astype(BF16), diag[3],
                          preferred_element_type=F32)
            acc = acc + jnp.dot(weights(low1, car), low1[3], preferred_element_type=F32)
            car = car + low1[1]
            acc_ref[u] = acc
            car_ref[u] = car
            tiles.append((idx[u], owns[u], qms[u], least_of(car)))

    for u, (i, own, qm, least0) in enumerate(tiles):
        n_low = lax.shift_right_logical(i + 2, 1)

        def step(start, mask, u=u, qm=qm):
            part = scores(qm, start, SB_CHUNK, mask)
            car_ = car_ref[u]
            acc_ref[u] += jnp.dot(weights(part, car_), part[3], preferred_element_type=F32)
            new_car = car_ + part[1]
            car_ref[u] = new_car
            return least_of(new_car)

        def cond(st, n_low=n_low):
            j, least_ = st
            return (j < n_low) & (least_ <= SB_SKIP_LOG)

        def body(st, own=own, step=step):
            j, _ = st
            return j + 1, step(pl.multiple_of(own - j * SB_CHUNK, BLOCK), None)

        j_end, least = lax.while_loop(cond, body, (jnp.int32(2), least0))

        @pl.when((n_low >= 2) & (j_end == n_low) & (least <= SB_SKIP_LOG))
        def _(own=own, n_low=n_low, step=step):
            step(pl.multiple_of(own - n_low * SB_CHUNK, BLOCK), "low")

        acc = acc_ref[u]
        o_ref[u * BLOCK:(u + 1) * BLOCK, :] = jnp.where(lo, acc[0:BLOCK], acc[BLOCK:2 * BLOCK])


def _sb_attention(proj, meta_blk, batch, seq):
    nq = seq // BLOCK
    ng = nq // SB_QB
    n_pairs = SB_WIDTH // LANES
    rows = SB_QB * BLOCK
    return pl.pallas_call(
        functools.partial(_sb_body, seq=seq),
        grid=(batch, n_pairs, ng),
        in_specs=[
            pl.BlockSpec((rows, LANES), lambda b, h, g: (b * ng + g, COL_Q_SB + h)),
            pl.BlockSpec((seq, LANES), lambda b, h, g: (b, COL_K_SB + h)),
            pl.BlockSpec((seq, LANES), lambda b, h, g: (b, COL_V_SB + h)),
            pl.BlockSpec((BLOCK, LANES), lambda b, h, g: (0, COL_K_SB + h)),
            pl.BlockSpec((BLOCK, LANES), lambda b, h, g: (0, COL_V_SB + h)),
        ],
        out_specs=pl.BlockSpec((rows, LANES), lambda b, h, g: (b * ng + g, h)),
        out_shape=jax.ShapeDtypeStruct((batch * seq, SB_WIDTH), F32),
        scratch_shapes=[
            pltpu.VMEM((seq + 2 * BLOCK, LANES), BF16),
            pltpu.VMEM((seq + 2 * BLOCK, LANES), BF16),
            pltpu.VMEM((SB_QB, 2 * BLOCK, LANES), F32),
            pltpu.VMEM((SB_QB, 2 * BLOCK, LANES), F32),
        ],
        compiler_params=_cparams(("parallel", "parallel", "arbitrary"), 32),
        name="sb_attn",
    )(proj, proj, proj, meta_blk, meta_blk)


SWA_QB = 16


def _swa_body(q_ref, kp_ref, kc_ref, vp_ref, vc_ref, km_ref, vm_ref, bkt_ref,
              rel_ref, sink_ref, o_ref, bias_ref):
    t = pl.program_id(0)
    g = pl.program_id(2)
    h0 = 2 * t
    kv_in_hi = (lax.shift_right_logical(t, 1) & 1) == 1

    @pl.when((pl.program_id(1) == 0) & (g == 0))
    def _():
        for tb in range(2):
            bkt = bkt_ref[tb]
            for hh in range(2):
                bias = jnp.full(bkt.shape, NEG_INF, F32)
                for b in range(NUM_BUCKETS):
                    bias = jnp.where(bkt == b, rel_ref[b, h0 + hh], bias)
                bias_ref[tb, hh * BLOCK:(hh + 1) * BLOCK, :] = bias

    lane = lax.broadcasted_iota(I32, (BLOCK, LANES), 1)
    lo = lane < HEAD_DIM
    kmask = jnp.logical_xor(lo, kv_in_hi)
    rcol = lax.broadcasted_iota(I32, (2 * BLOCK, 1), 0)
    sink = jnp.where(rcol < BLOCK, sink_ref[h0], sink_ref[h0 + 1])
    k_meta = km_ref[...]
    v_meta = vm_ref[...]

    blocks = range(SWA_QB)
    rows = [slice(u * BLOCK, (u + 1) * BLOCK) for u in blocks]
    scores, values = [], []
    for u in blocks:
        q = q_ref[rows[u], :]
        q_sw = jnp.concatenate([q[:, HEAD_DIM:], q[:, :HEAD_DIM]], axis=1)
        zq = jnp.zeros_like(q)
        qa = jnp.where(kmask, jnp.where(kv_in_hi, q_sw, q), zq)
        qb = jnp.where(kmask, jnp.where(kv_in_hi, q, q_sw), zq)
        qm = jnp.concatenate([qa, qb], axis=0)
        k_prev = kp_ref[...] if u == 0 else kc_ref[rows[u - 1], :]
        v_prev = vp_ref[...] if u == 0 else vc_ref[rows[u - 1], :]
        k_all = jnp.concatenate([k_prev, kc_ref[rows[u], :], k_meta], axis=0)
        values.append(jnp.concatenate([v_prev, vc_ref[rows[u], :], v_meta], axis=0))
        bias = bias_ref[jnp.minimum(g, 1)] if u == 0 else bias_ref[1]
        scores.append(lax.dot_general(qm, k_all, (((1,), (1,)), ((), ())), preferred_element_type=F32) + bias)
    peaks = [jnp.maximum(jnp.max(s, axis=-1, keepdims=True), sink) for s in scores]
    probs = [jnp.exp(s - m) for s, m in zip(scores, peaks)]
    denoms = [jnp.sum(p, axis=-1, keepdims=True) + jnp.exp(sink - m) for p, m in zip(probs, peaks)]
    outs = [jnp.dot((p * (1.0 / d)).astype(BF16), v, preferred_element_type=F32)
            for p, d, v in zip(probs, denoms, values)]
    for u in blocks:
        oa = outs[u][0:BLOCK]
        ob = outs[u][BLOCK:2 * BLOCK]
        oa_r = pltpu.roll(oa, HEAD_DIM, 1)
        ob_r = pltpu.roll(ob, HEAD_DIM, 1)
        o_ref[rows[u], :] = jnp.where(lo, jnp.where(kv_in_hi, oa_r, oa), jnp.where(kv_in_hi, ob, ob_r))


def _swa_attention(proj, meta_blk, bkt, rel_bias, sinks, batch, seq):
    nq = seq // BLOCK
    ng = nq // SWA_QB
    n_pairs = SWA_WIDTH // LANES
    rows = SWA_QB * BLOCK

    def cur(base):
        return lambda t, b, g: (b * ng + g, base + t // 4)

    def prev(base):
        return lambda t, b, g: (b * nq + jnp.maximum(SWA_QB * g - 1, 0), base + t // 4)

    smem = pl.BlockSpec(memory_space=pltpu.SMEM)
    return pl.pallas_call(
        _swa_body,
        grid=(n_pairs, batch, ng),
        in_specs=[
            pl.BlockSpec((rows, LANES), lambda t, b, g: (b * ng + g, COL_Q_SW + t)),
            pl.BlockSpec((BLOCK, LANES), prev(COL_K_SW)),
            pl.BlockSpec((rows, LANES), cur(COL_K_SW)),
            pl.BlockSpec((BLOCK, LANES), prev(COL_V_SW)),
            pl.BlockSpec((rows, LANES), cur(COL_V_SW)),
            pl.BlockSpec((BLOCK, LANES), lambda t, b, g: (0, COL_K_SW + t // 4)),
            pl.BlockSpec((BLOCK, LANES), lambda t, b, g: (0, COL_V_SW + t // 4)),
            pl.BlockSpec((2, BLOCK, 3 * BLOCK), lambda t, b, g: (0, 0, 0)),
            smem,
            smem,
        ],
        out_specs=pl.BlockSpec((rows, LANES), lambda t, b, g: (b * ng + g, t)),
        out_shape=jax.ShapeDtypeStruct((batch * seq, SWA_WIDTH), F32),
        scratch_shapes=[pltpu.VMEM((2, 2 * BLOCK, 3 * BLOCK), F32)],
        compiler_params=_cparams(("arbitrary", "arbitrary", "arbitrary"), 32),
        name="swa_attn",
    )(proj, proj, proj, proj, proj, meta_blk, meta_blk, bkt, rel_bias, sinks)


def _t5_bucket_np(dist):
    max_exact = NUM_BUCKETS // 2
    d = np.maximum(dist, 0)
    ratio = np.maximum(d, 1).astype(np.float32) / np.float32(max_exact)
    large = max_exact + (np.log(ratio) / np.float32(math.log(MAX_DISTANCE / max_exact))
                         * np.float32(NUM_BUCKETS - max_exact)).astype(np.int32)
    large = np.minimum(large, NUM_BUCKETS - 1)
    return np.where(d < max_exact, d, large).astype(np.int32)


def _bucket_tables():
    ql = np.arange(BLOCK)[:, None]
    sl = np.arange(2 * BLOCK)[None, :]
    dist = ql + BLOCK - sl
    ok = (dist >= 0) & (dist < WINDOW)
    band = np.where(ok, _t5_bucket_np(dist), -1).astype(np.int32)
    band_first = np.where(sl >= BLOCK, band, -1).astype(np.int32)
    ms = np.arange(BLOCK)[None, :]
    tables = []
    for blk, b in ((0, band_first), (1, band)):
        dist_m = ql + (blk + 1) * BLOCK - ms
        meta = np.where(ms >= META_PAD, _t5_bucket_np(dist_m), -1).astype(np.int32)
        tables.append(np.concatenate([b, meta], axis=1))
    return jnp.asarray(np.stack(tables))


def _out_router_body(osb_ref, osw_ref, x_ref, gsb_ref, gsw_ref, wo_ref, lf_ref, wr_ref, br_ref,
                     h1_ref, hfp_ref, route_ref):
    tm = x_ref.shape[0]
    n_groups = max(1, tm // TOK_TILE)
    rows = [slice(r * (tm // n_groups), (r + 1) * (tm // n_groups)) for r in range(n_groups)]

    def branch_norm(o_ref_, g_ref_, rs):
        a = o_ref_[rs, :]
        var = jnp.mean(a * a, axis=-1, keepdims=True)
        return ((a * lax.rsqrt(var + RMS_EPS)) * g_ref_[...]).astype(BF16)

    mixed = [jnp.concatenate([branch_norm(osb_ref, gsb_ref, rs), branch_norm(osw_ref, gsw_ref, rs)], axis=1)
             for rs in rows]
    h1s = [x_ref[rs, :] + jnp.dot(mx, wo_ref[...], preferred_element_type=F32) for mx, rs in zip(mixed, rows)]
    hfs = []
    for h1, rs in zip(h1s, rows):
        h1_ref[rs, :] = h1
        var = jnp.mean(h1 * h1, axis=-1, keepdims=True)
        hf = ((h1 * lax.rsqrt(var + RMS_EPS)) * lf_ref[...]).astype(BF16)
        hfs.append(hf)
        half = D_MODEL // 2
        lo_bits = pltpu.bitcast(hf[:, :half].astype(F32), U32)
        hi_bits = pltpu.bitcast(hf[:, half:].astype(F32), U32)
        _store_row_tiles(hfp_ref, rs.start, (hi_bits & jnp.uint32(0xFFFF0000)) | (lo_bits >> 16))
    all_logits = [jnp.dot(hf, wr_ref[...], preferred_element_type=F32) + br_ref[...] for hf in hfs]

    for logits, rs in zip(all_logits, rows):
        lane = lax.broadcasted_iota(I32, logits.shape, 1)
        big = jnp.int32(LANES)

        def softmax_masked(mask, logits=logits):
            z = jnp.where(mask, logits, NEG_INF)
            zmax = jnp.max(z, axis=-1, keepdims=True)
            e = jnp.exp(z - zmax)
            return e / jnp.sum(e, axis=-1, keepdims=True)

        def top1(p, mask, lane=lane):
            pm = jnp.where(mask, p, -1.0)
            v = jnp.max(pm, axis=-1, keepdims=True)
            idx = jnp.min(jnp.where(pm == v, lane, big), axis=-1, keepdims=True)
            return v, idx

        gmask = lane < N_GROUPS
        g_val, g_idx = top1(softmax_masked(gmask), gmask)
        e_lo = N_GROUPS + EXPERTS_PER_GROUP * g_idx
        emask = (lane >= e_lo) & (lane < e_lo + EXPERTS_PER_GROUP)
        e_prob = softmax_masked(emask)
        v1, i1 = top1(e_prob, emask)
        emask2 = emask & (lane != i1)
        v2, i2 = top1(e_prob, emask2)
        norm = v1 + v2
        gate1 = g_val * v1 / norm
        gate2 = g_val * v2 / norm
        ex1 = (i1 - N_GROUPS).astype(F32)
        ex2 = (i2 - N_GROUPS).astype(F32)
        route_ref[rs, :] = jnp.where(lane == 0, gate1,
                                     jnp.where(lane == 1, gate2,
                                               jnp.where(lane == 2, ex1,
                                                         jnp.where(lane == 3, ex2, 0.0))))


def _out_router(o_sb, o_sw, x2d, g_sb, g_sw, wo_bf, ln_ffn, w_router, b_router):
    m = x2d.shape[0]
    tm = min(OUT_TM, m)
    row = lambda w: pl.BlockSpec((tm, w), lambda i: (i, 0))
    const = lambda r, c: pl.BlockSpec((r, c), lambda i: (0, 0))
    return pl.pallas_call(
        _out_router_body,
        grid=(m // tm,),
        in_specs=[row(SB_WIDTH), row(SWA_WIDTH), row(D_MODEL), const(1, SB_WIDTH), const(1, SWA_WIDTH),
                  const(D_MODEL, D_MODEL), const(1, D_MODEL), const(D_MODEL, LANES), const(1, LANES)],
        out_specs=[row(D_MODEL), pl.BlockSpec((SUBLANES * tm, LANES), lambda i: (i, 0)), row(LANES)],
        out_shape=[jax.ShapeDtypeStruct((m, D_MODEL), F32),
                   jax.ShapeDtypeStruct((SUBLANES * m, LANES), U32),
                   jax.ShapeDtypeStruct((m, LANES), F32)],
        compiler_params=_cparams(("parallel",), 56),
        name="out_router",
    )(o_sb, o_sw, x2d, g_sb, g_sw, wo_bf, ln_ffn, w_router, b_router)


W_DMA_PRIORITY = 1


def _experts_body(te_ref, nx_ref, ws_ref, base_ref, nu_ref, tok_ref, hfp_ref, wg_ref, wu_ref, wd_ref, y_ref,
                  xbuf, wg_f, wu_f, wd_f, wg_bf, wu_bf, wd_bf, gsem, wsem):
    n = pl.program_id(0)
    n_used = nu_ref[0]

    def row_copy(tile, r, slot):
        tok = tok_ref[base_ref[tile] + r]
        return pltpu.make_async_copy(hfp_ref.at[pl.ds(pl.multiple_of(SUBLANES * tok, SUBLANES), SUBLANES), :],
                                     xbuf.at[slot, pl.ds(SUBLANES * r, SUBLANES), :], gsem.at[slot])

    def row_wait(r, slot):
        pltpu.make_async_copy(hfp_ref.at[pl.ds(0, SUBLANES), :], xbuf.at[slot, pl.ds(SUBLANES * r, SUBLANES), :],
                              gsem.at[slot]).wait()

    def weight_copies(e, slot):
        return (pltpu.make_async_copy(wg_ref.at[e], wg_f.at[slot], wsem.at[slot, 0]),
                pltpu.make_async_copy(wu_ref.at[e], wu_f.at[slot], wsem.at[slot, 1]),
                pltpu.make_async_copy(wd_ref.at[e], wd_f.at[slot], wsem.at[slot, 2]))

    @pl.when(n == 0)
    def _():
        for c in weight_copies(te_ref[0], 0):
            c.start(priority=W_DMA_PRIORITY)
        for r in range(MOE_TILE):
            row_copy(0, r, 0).start()

    @pl.when(n < n_used)
    def _():
        prev = te_ref[jnp.maximum(n - 1, 0)]
        fresh = (n == 0) | (te_ref[n] != prev)

        @pl.when(fresh)
        def _():
            ws = ws_ref[n]
            for c in weight_copies(te_ref[n], ws):
                c.wait()

            @pl.when(nx_ref[n] >= 0)
            def _():
                for c in weight_copies(nx_ref[n], 1 - ws):
                    c.start(priority=W_DMA_PRIORITY)

            wg_bf[...] = wg_f[ws].astype(BF16)
            wu_bf[...] = wu_f[ws].astype(BF16)
            wd_bf[...] = wd_f[ws].astype(BF16)

        def tile(slot):
            for r in range(MOE_TILE):
                row_copy(n + 1, r, 1 - slot).start()
            for r in range(MOE_TILE):
                row_wait(r, slot)

            u = _load_row_tiles(xbuf, MOE_TILE, (slot,))
            x_lo = pltpu.bitcast(u << 16, F32).astype(BF16)
            x_hi = pltpu.bitcast(u & jnp.uint32(0xFFFF0000), F32).astype(BF16)
            x = jnp.concatenate([x_lo, x_hi], axis=1)
            g = jnp.dot(x, wg_bf[...], preferred_element_type=F32)
            up = jnp.dot(x, wu_bf[...], preferred_element_type=F32)
            hdn = (g * jax.nn.sigmoid(g) * up).astype(BF16)
            y = jnp.dot(hdn, wd_bf[...], preferred_element_type=F32).astype(BF16)
            half = D_MODEL // 2
            y_lo = pltpu.bitcast(y[:, :half].astype(F32), U32)
            y_hi = pltpu.bitcast(y[:, half:].astype(F32), U32)
            _store_row_tiles(y_ref, 0, (y_hi & jnp.uint32(0xFFFF0000)) | (y_lo >> 16))

        for parity in range(2):
            pl.when((n & 1) == parity)(functools.partial(tile, parity))

    @pl.when(n == n_used)
    def _():
        for r in range(MOE_TILE):
            row_wait(r, n & 1)

    @pl.when(n >= n_used)
    def _():
        y_ref[...] = jnp.zeros_like(y_ref)


def _experts(tile_expert, next_expert, w_slot, tile_base, n_used, tok_sorted, hfp, w_gate, w_up, w_down):
    n_tiles = tile_expert.shape[0]
    any_space = pl.BlockSpec(memory_space=pl.ANY)
    grid_spec = pltpu.PrefetchScalarGridSpec(
        num_scalar_prefetch=6,
        grid=(n_tiles + 1,),
        in_specs=[any_space, any_space, any_space, any_space],
        out_specs=pl.BlockSpec((SUBLANES * MOE_TILE, LANES), lambda n, *_: (n, 0)),
        scratch_shapes=[pltpu.VMEM((2, SUBLANES * MOE_TILE, LANES), U32),
                        pltpu.VMEM((2, D_MODEL, D_EXPERT), F32),
                        pltpu.VMEM((2, D_MODEL, D_EXPERT), F32),
                        pltpu.VMEM((2, D_EXPERT, D_MODEL), F32),
                        pltpu.VMEM((D_MODEL, D_EXPERT), BF16),
                        pltpu.VMEM((D_MODEL, D_EXPERT), BF16),
                        pltpu.VMEM((D_EXPERT, D_MODEL), BF16),
                        pltpu.SemaphoreType.DMA((2,)),
                        pltpu.SemaphoreType.DMA((2, 3))],
    )
    return pl.pallas_call(
        _experts_body,
        grid_spec=grid_spec,
        out_shape=jax.ShapeDtypeStruct(((n_tiles + 1) * MOE_TILE * SUBLANES, LANES), U32),
        compiler_params=_cparams(("arbitrary",), 56),
        name="moe_experts",
    )(tile_expert, next_expert, w_slot, tile_base, n_used, tok_sorted, hfp, w_gate, w_up, w_down)


def _combine_body(dcur_ref, dnxt_ref, h1_ref, route_ref, y_ref, o_ref, ybuf, sem, *, n_steps):
    i = pl.program_id(0)
    n = dcur_ref.shape[2]

    def row_copy(idx_ref, a, slot):
        src_row = pl.multiple_of(SUBLANES * idx_ref[0, 0, a], SUBLANES)
        return pltpu.make_async_copy(y_ref.at[pl.ds(src_row, SUBLANES), :],
                                     ybuf.at[slot, a % 2, pl.ds(SUBLANES * (a // 2), SUBLANES), :], sem.at[slot])

    @pl.when(i == 0)
    def _():
        for a in range(n):
            row_copy(dcur_ref, a, 0).start()

    def row_wait(a, slot):
        pltpu.make_async_copy(y_ref.at[pl.ds(0, SUBLANES), :],
                              ybuf.at[slot, a % 2, pl.ds(SUBLANES * (a // 2), SUBLANES), :], sem.at[slot]).wait()

    def tile(slot):
        for a in range(n):
            row_copy(dnxt_ref, a, 1 - slot).start(priority=a % 2)
        for a in range(n):
            row_wait(a, slot)
        route = route_ref[...]
        g0 = route[:, 0:1]
        g1 = route[:, 1:2]
        u0 = _load_row_tiles(ybuf, n // 2, (slot, 0))
        u1 = _load_row_tiles(ybuf, n // 2, (slot, 1))
        half = D_MODEL // 2
        hi_mask = jnp.uint32(0xFFFF0000)
        o_ref[:, :half] = h1_ref[:, :half] + (pltpu.bitcast(u0 << 16, F32) * g0 + pltpu.bitcast(u1 << 16, F32) * g1)
        o_ref[:, half:] = h1_ref[:, half:] + (pltpu.bitcast(u0 & hi_mask, F32) * g0
                                              + pltpu.bitcast(u1 & hi_mask, F32) * g1)

    for parity in range(2):
        pl.when((i < n_steps) & ((i & 1) == parity))(functools.partial(tile, parity))

    @pl.when(i == n_steps)
    def _():
        for a in range(n):
            row_wait(a, i & 1)


def _combine(dest3, h1, route, y):
    m = h1.shape[0]
    tm = dest3.shape[2] // 2
    n_steps = m // tm
    tok = lambda i: (jnp.minimum(i, n_steps - 1), 0)
    return pl.pallas_call(
        functools.partial(_combine_body, n_steps=n_steps),
        grid=(n_steps + 1,),
        in_specs=[
            pl.BlockSpec((1, 1, 2 * tm), lambda i: (i, 0, 0), memory_space=pltpu.SMEM),
            pl.BlockSpec((1, 1, 2 * tm), lambda i: (jnp.minimum(i + 1, n_steps), 0, 0), memory_space=pltpu.SMEM),
            pl.BlockSpec((tm, D_MODEL), tok),
            pl.BlockSpec((tm, LANES), tok),
            pl.BlockSpec(memory_space=pl.ANY),
        ],
        out_specs=pl.BlockSpec((tm, D_MODEL), tok),
        out_shape=jax.ShapeDtypeStruct((m, D_MODEL), F32),
        scratch_shapes=[pltpu.VMEM((2, 2, SUBLANES * tm, LANES), U32), pltpu.SemaphoreType.DMA((2,))],
        compiler_params=_cparams(("arbitrary",), 40),
        name="moe_combine",
    )(dest3, dest3, h1, route, y)


def _routing_tables(route, n_tok):
    a_expert = route[:, 2:4].astype(I32).reshape(-1)
    n_assign = a_expert.shape[0]
    eids = jnp.arange(N_EXPERTS, dtype=I32)
    sorted_e, order = lax.sort((a_expert, jnp.arange(n_assign, dtype=I32)), num_keys=1, is_stable=True)
    first = jnp.sum((sorted_e[None, :] < eids[:, None]).astype(I32), axis=1)
    counts = jnp.concatenate([first[1:], jnp.full((1,), n_assign, I32)]) - first
    padded = (counts + MOE_TILE - 1) // MOE_TILE * MOE_TILE
    upto = (eids[None, :] <= eids[:, None]).astype(I32)
    seg_end = jnp.sum(upto * padded[None, :], axis=1)
    seg_start = seg_end - padded
    shift = seg_start - first
    shift_p = jnp.sum(jnp.where(sorted_e[None, :] == eids[:, None], shift[:, None], 0), axis=0)
    _, dest = lax.sort((order, jnp.arange(n_assign, dtype=I32) + shift_p), num_keys=1)

    n_tiles = n_assign // MOE_TILE + N_EXPERTS
    n_used = (seg_end[-1] // MOE_TILE).astype(I32)
    tile_start = jnp.arange(n_tiles + 2, dtype=I32) * MOE_TILE
    te = jnp.sum((seg_end[None, :] <= tile_start[:, None]).astype(I32), axis=1)
    te = jnp.minimum(te, N_EXPERTS - 1).astype(I32)
    used = counts > 0
    later_used = used[None, :] & (eids[None, :] > eids[:, None])
    next_e = jnp.min(jnp.where(later_used, eids[None, :], N_EXPERTS), axis=1)
    next_e = jnp.where(next_e == N_EXPERTS, -1, next_e).astype(I32)
    ordinal = jnp.sum(upto * used.astype(I32)[None, :], axis=1) - 1
    base = first[te] + tile_start - seg_start[te]
    base = jnp.where(jnp.arange(n_tiles + 2) < n_used, jnp.clip(base, 0, n_assign), n_assign).astype(I32)
    tok_sorted = jnp.concatenate([order // 2, jnp.zeros((MOE_TILE,), I32)])
    return (dest.astype(I32), tok_sorted, base, te[:n_tiles], next_e[te[:n_tiles]],
            (ordinal[te[:n_tiles]] & 1).astype(I32), n_used.reshape(1))


def kernel(x, meta_tokens, rel_bias, ln_mix, w_in, q_norm, k_norm, sinks, out_norm_sb, out_norm_swa,
           w_out, ln_ffn, w_router_group, b_router_group, w_router_expert, b_router_expert,
           w_gate, w_up, w_down):
    batch, seq, _ = x.shape
    n_tok = batch * seq
    x2d = x.reshape(n_tok, D_MODEL)

    ones = lambda n: jnp.ones((n,), F32)
    col_scale = jnp.concatenate([
        ones(SB_WIDTH) * SCALE, ones(2 * SB_WIDTH),
        jnp.tile(q_norm[0], SWA_Q_HEADS) * SCALE, jnp.tile(k_norm[0], SWA_KV_HEADS), ones(SWA_KV_WIDTH)])[None]
    norm_flag = jnp.concatenate([jnp.zeros((3 * SB_WIDTH,), F32), ones(SWA_WIDTH + SWA_KV_WIDTH),
                                 jnp.zeros((SWA_KV_WIDTH,), F32)])[None]
    w_in_bf = w_in[0].astype(BF16)
    gain_mix = ln_mix[0][None]

    proj = _norm_proj(x2d, gain_mix, w_in_bf, col_scale, norm_flag)
    meta_proj = _norm_proj(meta_tokens.astype(F32), gain_mix, w_in_bf, col_scale, norm_flag)
    meta_blk = jnp.concatenate([jnp.zeros((META_PAD, IN_PROJ_WIDTH), BF16), meta_proj], axis=0)

    o_sb = _sb_attention(proj, meta_blk, batch, seq)
    o_sw = _swa_attention(proj, meta_blk, _bucket_tables(), rel_bias.astype(F32), sinks[0].astype(F32),
                          batch, seq)

    w_router = jnp.concatenate(
        [w_router_group[0], w_router_expert[0],
         jnp.zeros((D_MODEL, LANES - N_GROUPS - N_EXPERTS), F32)], axis=1).astype(BF16)
    b_router = jnp.concatenate(
        [b_router_group[0], b_router_expert[0], jnp.zeros((LANES - N_GROUPS - N_EXPERTS,), F32)])[None]
    h1, hfp, route = _out_router(o_sb, o_sw, x2d, out_norm_sb[0][None], out_norm_swa[0][None],
                                 w_out[0].astype(BF16), ln_ffn[0][None], w_router, b_router)

    dest, tok_sorted, tile_base, tile_expert, next_expert, w_slot, n_used = _routing_tables(route, n_tok)
    y = _experts(tile_expert, next_expert, w_slot, tile_base, n_used, tok_sorted, hfp,
                 w_gate[0], w_up[0], w_down[0])
    dest3 = jnp.concatenate([dest, jnp.zeros((2 * TOK_TILE,), I32)]).reshape(-1, 1, 2 * TOK_TILE)
    out = _combine(dest3, h1, route, y)
    return out.reshape(batch, seq, D_MODEL)
```

```python
import functools
import math

import numpy as np
import jax
import jax.numpy as jnp
from jax import lax
from jax.experimental import pallas as pl
from jax.experimental.pallas import tpu as pltpu

F32 = jnp.float32
BF16 = jnp.bfloat16
I32 = jnp.int32
U32 = jnp.uint32

D_MODEL = 2048
HEAD_DIM = 64
LANES = 128
SB_HEADS = 16
SWA_Q_HEADS = 16
SWA_KV_HEADS = 4
SB_WIDTH = SB_HEADS * HEAD_DIM
SWA_WIDTH = SWA_Q_HEADS * HEAD_DIM
SWA_KV_WIDTH = SWA_KV_HEADS * HEAD_DIM
IN_PROJ_WIDTH = 3 * SB_WIDTH + SWA_WIDTH + 2 * SWA_KV_WIDTH
BLOCK = 128
WINDOW = 128
N_META = 16
META_PAD = BLOCK - N_META
NUM_BUCKETS = 32
MAX_DISTANCE = 128
N_GROUPS = 8
EXPERTS_PER_GROUP = 8
N_EXPERTS = N_GROUPS * EXPERTS_PER_GROUP
D_EXPERT = D_MODEL // 4
RMS_EPS = 1e-6
SCALE = HEAD_DIM ** -0.5
LOG2E = 1.4426950408889634

COL_Q_SB = 0
COL_K_SB = SB_WIDTH // LANES
COL_V_SB = 2 * SB_WIDTH // LANES
COL_Q_SW = 3 * SB_WIDTH // LANES
COL_K_SW = COL_Q_SW + SWA_WIDTH // LANES
COL_V_SW = COL_K_SW + SWA_KV_WIDTH // LANES

PROJ_TN = 768
PROJ_TM = 512
OUT_TM = 512
SB_CHUNK = 256
MOE_TILE = 256
TOK_TILE = 256
NEG_INF = float("-inf")


def _cparams(sem, vmem_mb):
    return pltpu.CompilerParams(dimension_semantics=sem, vmem_limit_bytes=vmem_mb * 1024 * 1024)


SUBLANES = 8
ROW_WORDS = D_MODEL // 2
assert ROW_WORDS == SUBLANES * LANES


def _store_row_tiles(ref, row0, value, lead=()):
    rows = value.shape[0]
    for c in range(SUBLANES):
        ref[lead + (pl.ds(SUBLANES * row0 + c, rows, stride=SUBLANES), slice(None))] = \
            value[:, c * LANES:(c + 1) * LANES]


def _load_row_tiles(ref, rows, lead=()):
    return jnp.concatenate(
        [ref[lead + (pl.ds(c, rows, stride=SUBLANES), slice(None))] for c in range(SUBLANES)], axis=1)


def _norm_proj_body(x_ref, g_ref, w_ref, cs_ref, nf_ref, o_ref, *, n_plain_tiles):
    x = x_ref[...]
    var = jnp.mean(x * x, axis=-1, keepdims=True)
    hn = (x * lax.rsqrt(var + RMS_EPS) * g_ref[...]).astype(BF16)
    lane = lax.broadcasted_iota(I32, (1, LANES), 1)
    lo = lane < HEAD_DIM
    for j in range(w_ref.shape[1] // PROJ_TN):
        cols = slice(j * PROJ_TN, (j + 1) * PROJ_TN)
        acc = jnp.dot(hn, w_ref[:, cols], preferred_element_type=F32)
        if j < n_plain_tiles:
            o_ref[:, cols] = (acc * cs_ref[:, cols]).astype(o_ref.dtype)
            continue
        for c in range(PROJ_TN // LANES):
            sl = slice(j * PROJ_TN + c * LANES, j * PROJ_TN + (c + 1) * LANES)
            a = acc[:, c * LANES:(c + 1) * LANES]
            s = a * a
            s_lo = jnp.sum(jnp.where(lo, s, 0.0), axis=-1, keepdims=True)
            s_hi = jnp.sum(jnp.where(lo, 0.0, s), axis=-1, keepdims=True)
            ms = jnp.where(lo, s_lo, s_hi) * (1.0 / HEAD_DIM)
            r = jnp.where(nf_ref[:, sl] > 0.0, lax.rsqrt(ms + RMS_EPS), 1.0)
            o_ref[:, sl] = ((a * r) * cs_ref[:, sl]).astype(o_ref.dtype)


def _norm_proj(x2d, gain, w_bf, col_scale, norm_flag):
    m = x2d.shape[0]
    tm = min(PROJ_TM, m)
    n = w_bf.shape[1]
    n_plain = (3 * SB_WIDTH) // PROJ_TN
    whole = lambda r, c: pl.BlockSpec((r, c), lambda i: (0, 0))
    return pl.pallas_call(
        functools.partial(_norm_proj_body, n_plain_tiles=n_plain),
        grid=(m // tm,),
        in_specs=[pl.BlockSpec((tm, D_MODEL), lambda i: (i, 0)), whole(1, D_MODEL), whole(D_MODEL, n),
                  whole(1, n), whole(1, n)],
        out_specs=pl.BlockSpec((tm, n), lambda i: (i, 0)),
        out_shape=jax.ShapeDtypeStruct((m, n), BF16),
        compiler_params=_cparams(("parallel",), 56),
        name="norm_proj",
    )(x2d, gain, w_bf, col_scale, norm_flag)


SB_SKIP_LOG = 104.0


SB_QB = 8
SB_STAGE_GROUP = 4


def _sb_body(q_ref, k_ref, v_ref, km_ref, vm_ref, o_ref, kx_ref, vx_ref, acc_ref, car_ref, *, seq):
    g = pl.program_id(2)

    @pl.when(g == 0)
    def _():
        zeros = jnp.zeros((BLOCK, LANES), BF16)
        kx_ref[0:BLOCK, :] = zeros
        vx_ref[0:BLOCK, :] = zeros
        kx_ref[BLOCK:2 * BLOCK, :] = km_ref[...]
        vx_ref[BLOCK:2 * BLOCK, :] = vm_ref[...]
        kx_ref[2 * BLOCK:, :] = k_ref[...]
        vx_ref[2 * BLOCK:, :] = v_ref[...]

    lane = lax.broadcasted_iota(I32, (BLOCK, LANES), 1)
    lo = lane < HEAD_DIM

    def later_key_matrix(nk):
        r = lax.broadcasted_iota(I32, (nk, nk), 0)
        c = lax.broadcasted_iota(I32, (nk, nk), 1)
        return (r > c).astype(F32).astype(BF16)

    tri = {BLOCK: later_key_matrix(BLOCK), SB_CHUNK: later_key_matrix(SB_CHUNK)}

    def stage_qk(qm, start, nk):
        kc = kx_ref[pl.ds(start, nk), :]
        return lax.dot_general(qm, kc, (((1,), (1,)), ((), ())), preferred_element_type=F32)

    def stage_soft(s, start, mask):
        nk = s.shape[1]
        sp = jnp.maximum(s, 0.0) + jnp.log(1.0 + jnp.exp2(jnp.abs(s) * (-LOG2E)))
        log_beta = s - sp
        valid = None
        if mask is not None:
            r_i = lax.broadcasted_iota(I32, (2 * BLOCK, nk), 0)
            c_i = lax.broadcasted_iota(I32, (2 * BLOCK, nk), 1)
            if mask == "diag":
                valid = c_i < (r_i & (BLOCK - 1))
                sp = jnp.where(valid, sp, 0.0)
            else:
                sp = jnp.where((start + c_i) >= (BLOCK + META_PAD), sp, 0.0)
        return sp, log_beta, valid

    def stage_inner(soft, start):
        sp, log_beta, valid = soft
        nk = sp.shape[1]
        inner = jnp.dot(sp.astype(BF16), tri[nk], preferred_element_type=F32)
        return log_beta - inner, jnp.sum(sp, axis=1, keepdims=True), valid, vx_ref[pl.ds(start, nk), :]

    def scores(qm, start, nk, mask):
        return stage_inner(stage_soft(stage_qk(qm, start, nk), start, mask), start)

    def weights(part, car):
        log_w0, _, valid, _ = part
        nk = log_w0.shape[1]
        car_k = car if nk == LANES else jnp.concatenate([car, car], axis=1)
        w = jnp.exp(log_w0 - car_k)
        if valid is not None:
            w = jnp.where(valid, w, 0.0)
        return w.astype(BF16)

    def least_of(car):
        return jnp.min(jnp.min(car, axis=0, keepdims=True))

    idx, owns, qms = [], [], []
    for u in range(SB_QB):
        i = SB_QB * g + u
        idx.append(i)
        owns.append(pl.multiple_of(BLOCK * (i + 2), BLOCK))
        q = q_ref[u * BLOCK:(u + 1) * BLOCK, :]
        zq = jnp.zeros_like(q)
        qms.append(jnp.concatenate([jnp.where(lo, q, zq), jnp.where(lo, zq, q)], axis=0))
    tiles = []
    for u0 in range(0, SB_QB, SB_STAGE_GROUP):
        group = range(u0, min(u0 + SB_STAGE_GROUP, SB_QB))
        chunks = []
        for u in group:
            chunks.append((u, owns[u], BLOCK, "diag"))
            chunks.append((u, pl.multiple_of(owns[u] - SB_CHUNK, BLOCK), SB_CHUNK, "low"))
        qk = [stage_qk(qms[u], start, nk) for u, start, nk, _ in chunks]
        soft = [stage_soft(s, start, mask) for s, (_, start, _, mask) in zip(qk, chunks)]
        parts = [stage_inner(sf, start) for sf, (_, start, _, _) in zip(soft, chunks)]
        for n, u in enumerate(group):
            diag, low1 = parts[2 * n], parts[2 * n + 1]
            car = jnp.broadcast_to(diag[1], (2 * BLOCK, LANES))
            acc = jnp.dot(jnp.where(diag[2], jnp.exp(diag[0]), 0.0).astype(BF16), diag[3],
                          preferred_element_type=F32)
            acc = acc + jnp.dot(weights(low1, car), low1[3], preferred_element_type=F32)
            car = car + low1[1]
            acc_ref[u] = acc
            car_ref[u] = car
            tiles.append((idx[u], owns[u], qms[u], least_of(car)))

    for u, (i, own, qm, least0) in enumerate(tiles):
        n_low = lax.shift_right_logical(i + 2, 1)

        def step(start, mask, u=u, qm=qm):
            part = scores(qm, start, SB_CHUNK, mask)
            car_ = car_ref[u]
            acc_ref[u] += jnp.dot(weights(part, car_), part[3], preferred_element_type=F32)
            new_car = car_ + part[1]
            car_ref[u] = new_car
            return least_of(new_car)

        def cond(st, n_low=n_low):
            j, least_ = st
            return (j < n_low) & (least_ <= SB_SKIP_LOG)

        def body(st, own=own, step=step):
            j, _ = st
            return j + 1, step(pl.multiple_of(own - j * SB_CHUNK, BLOCK), None)

        j_end, least = lax.while_loop(cond, body, (jnp.int32(2), least0))

        @pl.when((n_low >= 2) & (j_end == n_low) & (least <= SB_SKIP_LOG))
        def _(own=own, n_low=n_low, step=step):
            step(pl.multiple_of(own - n_low * SB_CHUNK, BLOCK), "low")

        acc = acc_ref[u]
        o_ref[u * BLOCK:(u + 1) * BLOCK, :] = jnp.where(lo, acc[0:BLOCK], acc[BLOCK:2 * BLOCK])


def _sb_attention(proj, meta_blk, batch, seq):
    nq = seq // BLOCK
    ng = nq // SB_QB
    n_pairs = SB_WIDTH // LANES
    rows = SB_QB * BLOCK
    return pl.pallas_call(
        functools.partial(_sb_body, seq=seq),
        grid=(batch, n_pairs, ng),
        in_specs=[
            pl.BlockSpec((rows, LANES), lambda b, h, g: (b * ng + g, COL_Q_SB + h)),
            pl.BlockSpec((seq, LANES), lambda b, h, g: (b, COL_K_SB + h)),
            pl.BlockSpec((seq, LANES), lambda b, h, g: (b, COL_V_SB + h)),
            pl.BlockSpec((BLOCK, LANES), lambda b, h, g: (0, COL_K_SB + h)),
            pl.BlockSpec((BLOCK, LANES), lambda b, h, g: (0, COL_V_SB + h)),
        ],
        out_specs=pl.BlockSpec((rows, LANES), lambda b, h, g: (b * ng + g, h)),
        out_shape=jax.ShapeDtypeStruct((batch * seq, SB_WIDTH), F32),
        scratch_shapes=[
            pltpu.VMEM((seq + 2 * BLOCK, LANES), BF16),
            pltpu.VMEM((seq + 2 * BLOCK, LANES), BF16),
            pltpu.VMEM((SB_QB, 2 * BLOCK, LANES), F32),
            pltpu.VMEM((SB_QB, 2 * BLOCK, LANES), F32),
        ],
        compiler_params=_cparams(("parallel", "parallel", "arbitrary"), 32),
        name="sb_attn",
    )(proj, proj, proj, meta_blk, meta_blk)


SWA_QB = 32


def _swa_body(q_ref, kp_ref, kc_ref, vp_ref, vc_ref, km_ref, vm_ref, bkt_ref,
              rel_ref, sink_ref, o_ref, bias_ref):
    t = pl.program_id(0)
    g = pl.program_id(2)
    h0 = 2 * t
    kv_in_hi = (lax.shift_right_logical(t, 1) & 1) == 1

    @pl.when((pl.program_id(1) == 0) & (g == 0))
    def _():
        for tb in range(2):
            bkt = bkt_ref[tb]
            for hh in range(2):
                bias = jnp.full(bkt.shape, NEG_INF, F32)
                for b in range(NUM_BUCKETS):
                    bias = jnp.where(bkt == b, rel_ref[b, h0 + hh], bias)
                bias_ref[tb, hh * BLOCK:(hh + 1) * BLOCK, :] = bias

    lane = lax.broadcasted_iota(I32, (BLOCK, LANES), 1)
    lo = lane < HEAD_DIM
    kmask = jnp.logical_xor(lo, kv_in_hi)
    rcol = lax.broadcasted_iota(I32, (2 * BLOCK, 1), 0)
    sink = jnp.where(rcol < BLOCK, sink_ref[h0], sink_ref[h0 + 1])
    k_meta = km_ref[...]
    v_meta = vm_ref[...]

    blocks = range(SWA_QB)
    rows = [slice(u * BLOCK, (u + 1) * BLOCK) for u in blocks]
    scores, values = [], []
    for u in blocks:
        q = q_ref[rows[u], :]
        q_sw = jnp.concatenate([q[:, HEAD_DIM:], q[:, :HEAD_DIM]], axis=1)
        zq = jnp.zeros_like(q)
        qa = jnp.where(kmask, jnp.where(kv_in_hi, q_sw, q), zq)
        qb = jnp.where(kmask, jnp.where(kv_in_hi, q, q_sw), zq)
        qm = jnp.concatenate([qa, qb], axis=0)
        k_prev = kp_ref[...] if u == 0 else kc_ref[rows[u - 1], :]
        v_prev = vp_ref[...] if u == 0 else vc_ref[rows[u - 1], :]
        k_all = jnp.concatenate([k_prev, kc_ref[rows[u], :], k_meta], axis=0)
        values.append(jnp.concatenate([v_prev, vc_ref[rows[u], :], v_meta], axis=0))
        bias = bias_ref[jnp.minimum(g, 1)] if u == 0 else bias_ref[1]
        scores.append(lax.dot_general(qm, k_all, (((1,), (1,)), ((), ())), preferred_element_type=F32) + bias)
    peaks = [jnp.maximum(jnp.max(s, axis=-1, keepdims=True), sink) for s in scores]
    probs = [jnp.exp(s - m) for s, m in zip(scores, peaks)]
    denoms = [jnp.sum(p, axis=-1, keepdims=True) + jnp.exp(sink - m) for p, m in zip(probs, peaks)]
    outs = [jnp.dot((p * (1.0 / d)).astype(BF16), v, preferred_element_type=F32)
            for p, d, v in zip(probs, denoms, values)]
    for u in blocks:
        oa = outs[u][0:BLOCK]
        ob = outs[u][BLOCK:2 * BLOCK]
        oa_r = pltpu.roll(oa, HEAD_DIM, 1)
        ob_r = pltpu.roll(ob, HEAD_DIM, 1)
        o_ref[rows[u], :] = jnp.where(lo, jnp.where(kv_in_hi, oa_r, oa), jnp.where(kv_in_hi, ob, ob_r))


def _swa_attention(proj, meta_blk, bkt, rel_bias, sinks, batch, seq):
    nq = seq // BLOCK
    ng = nq // SWA_QB
    n_pairs = SWA_WIDTH // LANES
    rows = SWA_QB * BLOCK

    def cur(base):
        return lambda t, b, g: (b * ng + g, base + t // 4)

    def prev(base):
        return lambda t, b, g: (b * nq + jnp.maximum(SWA_QB * g - 1, 0), base + t // 4)

    smem = pl.BlockSpec(memory_space=pltpu.SMEM)
    return pl.pallas_call(
        _swa_body,
        grid=(n_pairs, batch, ng),
        in_specs=[
            pl.BlockSpec((rows, LANES), lambda t, b, g: (b * ng + g, COL_Q_SW + t)),
            pl.BlockSpec((BLOCK, LANES), prev(COL_K_SW)),
            pl.BlockSpec((rows, LANES), cur(COL_K_SW)),
            pl.BlockSpec((BLOCK, LANES), prev(COL_V_SW)),
            pl.BlockSpec((rows, LANES), cur(COL_V_SW)),
            pl.BlockSpec((BLOCK, LANES), lambda t, b, g: (0, COL_K_SW + t // 4)),
            pl.BlockSpec((BLOCK, LANES), lambda t, b, g: (0, COL_V_SW + t // 4)),
            pl.BlockSpec((2, BLOCK, 3 * BLOCK), lambda t, b, g: (0, 0, 0)),
            smem,
            smem,
        ],
        out_specs=pl.BlockSpec((rows, LANES), lambda t, b, g: (b * ng + g, t)),
        out_shape=jax.ShapeDtypeStruct((batch * seq, SWA_WIDTH), F32),
        scratch_shapes=[pltpu.VMEM((2, 2 * BLOCK, 3 * BLOCK), F32)],
        compiler_params=_cparams(("arbitrary", "arbitrary", "arbitrary"), 32),
        name="swa_attn",
    )(proj, proj, proj, proj, proj, meta_blk, meta_blk, bkt, rel_bias, sinks)


def _t5_bucket_np(dist):
    max_exact = NUM_BUCKETS // 2
    d = np.maximum(dist, 0)
    ratio = np.maximum(d, 1).astype(np.float32) / np.float32(max_exact)
    large = max_exact + (np.log(ratio) / np.float32(math.log(MAX_DISTANCE / max_exact))
                         * np.float32(NUM_BUCKETS - max_exact)).astype(np.int32)
    large = np.minimum(large, NUM_BUCKETS - 1)
    return np.where(d < max_exact, d, large).astype(np.int32)


def _bucket_tables():
    ql = np.arange(BLOCK)[:, None]
    sl = np.arange(2 * BLOCK)[None, :]
    dist = ql + BLOCK - sl
    ok = (dist >= 0) & (dist < WINDOW)
    band = np.where(ok, _t5_bucket_np(dist), -1).astype(np.int32)
    band_first = np.where(sl >= BLOCK, band, -1).astype(np.int32)
    ms = np.arange(BLOCK)[None, :]
    tables = []
    for blk, b in ((0, band_first), (1, band)):
        dist_m = ql + (blk + 1) * BLOCK - ms
        meta = np.where(ms >= META_PAD, _t5_bucket_np(dist_m), -1).astype(np.int32)
        tables.append(np.concatenate([b, meta], axis=1))
    return jnp.asarray(np.stack(tables))


def _out_router_body(osb_ref, osw_ref, x_ref, gsb_ref, gsw_ref, wo_ref, lf_ref, wr_ref, br_ref,
                     h1_ref, hfp_ref, route_ref):
    tm = x_ref.shape[0]
    n_groups = max(1, tm // TOK_TILE)
    rows = [slice(r * (tm // n_groups), (r + 1) * (tm // n_groups)) for r in range(n_groups)]

    def branch_norm(o_ref_, g_ref_, rs):
        a = o_ref_[rs, :]
        var = jnp.mean(a * a, axis=-1, keepdims=True)
        return ((a * lax.rsqrt(var + RMS_EPS)) * g_ref_[...]).astype(BF16)

    mixed = [jnp.concatenate([branch_norm(osb_ref, gsb_ref, rs), branch_norm(osw_ref, gsw_ref, rs)], axis=1)
             for rs in rows]
    h1s = [x_ref[rs, :] + jnp.dot(mx, wo_ref[...], preferred_element_type=F32) for mx, rs in zip(mixed, rows)]
    hfs = []
    for h1, rs in zip(h1s, rows):
        h1_ref[rs, :] = h1
        var = jnp.mean(h1 * h1, axis=-1, keepdims=True)
        hf = ((h1 * lax.rsqrt(var + RMS_EPS)) * lf_ref[...]).astype(BF16)
        hfs.append(hf)
        half = D_MODEL // 2
        lo_bits = pltpu.bitcast(hf[:, :half].astype(F32), U32)
        hi_bits = pltpu.bitcast(hf[:, half:].astype(F32), U32)
        _store_row_tiles(hfp_ref, rs.start, (hi_bits & jnp.uint32(0xFFFF0000)) | (lo_bits >> 16))
    all_logits = [jnp.dot(hf, wr_ref[...], preferred_element_type=F32) + br_ref[...] for hf in hfs]

    for logits, rs in zip(all_logits, rows):
        lane = lax.broadcasted_iota(I32, logits.shape, 1)
        big = jnp.int32(LANES)

        def softmax_masked(mask, logits=logits):
            z = jnp.where(mask, logits, NEG_INF)
            zmax = jnp.max(z, axis=-1, keepdims=True)
            e = jnp.exp(z - zmax)
            return e / jnp.sum(e, axis=-1, keepdims=True)

        def top1(p, mask, lane=lane):
            pm = jnp.where(mask, p, -1.0)
            v = jnp.max(pm, axis=-1, keepdims=True)
            idx = jnp.min(jnp.where(pm == v, lane, big), axis=-1, keepdims=True)
            return v, idx

        gmask = lane < N_GROUPS
        g_val, g_idx = top1(softmax_masked(gmask), gmask)
        e_lo = N_GROUPS + EXPERTS_PER_GROUP * g_idx
        emask = (lane >= e_lo) & (lane < e_lo + EXPERTS_PER_GROUP)
        e_prob = softmax_masked(emask)
        v1, i1 = top1(e_prob, emask)
        emask2 = emask & (lane != i1)
        v2, i2 = top1(e_prob, emask2)
        norm = v1 + v2
        gate1 = g_val * v1 / norm
        gate2 = g_val * v2 / norm
        ex1 = (i1 - N_GROUPS).astype(F32)
        ex2 = (i2 - N_GROUPS).astype(F32)
        route_ref[rs, :] = jnp.where(lane == 0, gate1,
                                     jnp.where(lane == 1, gate2,
                                               jnp.where(lane == 2, ex1,
                                                         jnp.where(lane == 3, ex2, 0.0))))


def _out_router(o_sb, o_sw, x2d, g_sb, g_sw, wo_bf, ln_ffn, w_router, b_router):
    m = x2d.shape[0]
    tm = min(OUT_TM, m)
    row = lambda w: pl.BlockSpec((tm, w), lambda i: (i, 0))
    const = lambda r, c: pl.BlockSpec((r, c), lambda i: (0, 0))
    return pl.pallas_call(
        _out_router_body,
        grid=(m // tm,),
        in_specs=[row(SB_WIDTH), row(SWA_WIDTH), row(D_MODEL), const(1, SB_WIDTH), const(1, SWA_WIDTH),
                  const(D_MODEL, D_MODEL), const(1, D_MODEL), const(D_MODEL, LANES), const(1, LANES)],
        out_specs=[row(D_MODEL), pl.BlockSpec((SUBLANES * tm, LANES), lambda i: (i, 0)), row(LANES)],
        out_shape=[jax.ShapeDtypeStruct((m, D_MODEL), F32),
                   jax.ShapeDtypeStruct((SUBLANES * m, LANES), U32),
                   jax.ShapeDtypeStruct((m, LANES), F32)],
        compiler_params=_cparams(("parallel",), 56),
        name="out_router",
    )(o_sb, o_sw, x2d, g_sb, g_sw, wo_bf, ln_ffn, w_router, b_router)


W_DMA_PRIORITY = 1


def _experts_body(te_ref, nx_ref, ws_ref, base_ref, nu_ref, tok_ref, hfp_ref, wg_ref, wu_ref, wd_ref, y_ref,
                  xbuf, wg_f, wu_f, wd_f, wg_bf, wu_bf, wd_bf, gsem, wsem):
    n = pl.program_id(0)
    n_used = nu_ref[0]

    def row_copy(tile, r, slot):
        tok = tok_ref[base_ref[tile] + r]
        return pltpu.make_async_copy(hfp_ref.at[pl.ds(pl.multiple_of(SUBLANES * tok, SUBLANES), SUBLANES), :],
                                     xbuf.at[slot, pl.ds(SUBLANES * r, SUBLANES), :], gsem.at[slot])

    def row_wait(r, slot):
        pltpu.make_async_copy(hfp_ref.at[pl.ds(0, SUBLANES), :], xbuf.at[slot, pl.ds(SUBLANES * r, SUBLANES), :],
                              gsem.at[slot]).wait()

    def weight_copies(e, slot):
        return (pltpu.make_async_copy(wg_ref.at[e], wg_f.at[slot], wsem.at[slot, 0]),
                pltpu.make_async_copy(wu_ref.at[e], wu_f.at[slot], wsem.at[slot, 1]),
                pltpu.make_async_copy(wd_ref.at[e], wd_f.at[slot], wsem.at[slot, 2]))

    @pl.when(n == 0)
    def _():
        for c in weight_copies(te_ref[0], 0):
            c.start(priority=W_DMA_PRIORITY)
        for r in range(MOE_TILE):
            row_copy(0, r, 0).start()

    @pl.when(n < n_used)
    def _():
        prev = te_ref[jnp.maximum(n - 1, 0)]
        fresh = (n == 0) | (te_ref[n] != prev)

        @pl.when(fresh)
        def _():
            ws = ws_ref[n]
            for c in weight_copies(te_ref[n], ws):
                c.wait()

            @pl.when(nx_ref[n] >= 0)
            def _():
                for c in weight_copies(nx_ref[n], 1 - ws):
                    c.start(priority=W_DMA_PRIORITY)

            wg_bf[...] = wg_f[ws].astype(BF16)
            wu_bf[...] = wu_f[ws].astype(BF16)
            wd_bf[...] = wd_f[ws].astype(BF16)

        def tile(slot):
            for r in range(MOE_TILE):
                row_copy(n + 1, r, 1 - slot).start()
            for r in range(MOE_TILE):
                row_wait(r, slot)

            u = _load_row_tiles(xbuf, MOE_TILE, (slot,))
            x_lo = pltpu.bitcast(u << 16, F32).astype(BF16)
            x_hi = pltpu.bitcast(u & jnp.uint32(0xFFFF0000), F32).astype(BF16)
            x = jnp.concatenate([x_lo, x_hi], axis=1)
            g = jnp.dot(x, wg_bf[...], preferred_element_type=F32)
            up = jnp.dot(x, wu_bf[...], preferred_element_type=F32)
            hdn = (g * jax.nn.sigmoid(g) * up).astype(BF16)
            y = jnp.dot(hdn, wd_bf[...], preferred_element_type=F32).astype(BF16)
            half = D_MODEL // 2
            y_lo = pltpu.bitcast(y[:, :half].astype(F32), U32)
            y_hi = pltpu.bitcast(y[:, half:].astype(F32), U32)
            _store_row_tiles(y_ref, 0, (y_hi & jnp.uint32(0xFFFF0000)) | (y_lo >> 16))

        for parity in range(2):
            pl.when((n & 1) == parity)(functools.partial(tile, parity))

    @pl.when(n == n_used)
    def _():
        for r in range(MOE_TILE):
            row_wait(r, n & 1)

    @pl.when(n >= n_used)
    def _():
        y_ref[...] = jnp.zeros_like(y_ref)


def _experts(tile_expert, next_expert, w_slot, tile_base, n_used, tok_sorted, hfp, w_gate, w_up, w_down):
    n_tiles = tile_expert.shape[0]
    any_space = pl.BlockSpec(memory_space=pl.ANY)
    grid_spec = pltpu.PrefetchScalarGridSpec(
        num_scalar_prefetch=6,
        grid=(n_tiles + 1,),
        in_specs=[any_space, any_space, any_space, any_space],
        out_specs=pl.BlockSpec((SUBLANES * MOE_TILE, LANES), lambda n, *_: (n, 0)),
        scratch_shapes=[pltpu.VMEM((2, SUBLANES * MOE_TILE, LANES), U32),
                        pltpu.VMEM((2, D_MODEL, D_EXPERT), F32),
                        pltpu.VMEM((2, D_MODEL, D_EXPERT), F32),
                        pltpu.VMEM((2, D_EXPERT, D_MODEL), F32),
                        pltpu.VMEM((D_MODEL, D_EXPERT), BF16),
                        pltpu.VMEM((D_MODEL, D_EXPERT), BF16),
                        pltpu.VMEM((D_EXPERT, D_MODEL), BF16),
                        pltpu.SemaphoreType.DMA((2,)),
                        pltpu.SemaphoreType.DMA((2, 3))],
    )
    return pl.pallas_call(
        _experts_body,
        grid_spec=grid_spec,
        out_shape=jax.ShapeDtypeStruct(((n_tiles + 1) * MOE_TILE * SUBLANES, LANES), U32),
        compiler_params=_cparams(("arbitrary",), 56),
        name="moe_experts",
    )(tile_expert, next_expert, w_slot, tile_base, n_used, tok_sorted, hfp, w_gate, w_up, w_down)


def _combine_body(dcur_ref, dnxt_ref, h1_ref, route_ref, y_ref, o_ref, ybuf, sem, *, n_steps):
    i = pl.program_id(0)
    n = dcur_ref.shape[2]

    def row_copy(idx_ref, a, slot):
        src_row = pl.multiple_of(SUBLANES * idx_ref[0, 0, a], SUBLANES)
        return pltpu.make_async_copy(y_ref.at[pl.ds(src_row, SUBLANES), :],
                                     ybuf.at[slot, a % 2, pl.ds(SUBLANES * (a // 2), SUBLANES), :], sem.at[slot])

    @pl.when(i == 0)
    def _():
        for a in range(n):
            row_copy(dcur_ref, a, 0).start()

    def row_wait(a, slot):
        pltpu.make_async_copy(y_ref.at[pl.ds(0, SUBLANES), :],
                              ybuf.at[slot, a % 2, pl.ds(SUBLANES * (a // 2), SUBLANES), :], sem.at[slot]).wait()

    def tile(slot):
        for a in range(n):
            row_copy(dnxt_ref, a, 1 - slot).start(priority=a % 2)
        for a in range(n):
            row_wait(a, slot)
        route = route_ref[...]
        g0 = route[:, 0:1]
        g1 = route[:, 1:2]
        u0 = _load_row_tiles(ybuf, n // 2, (slot, 0))
        u1 = _load_row_tiles(ybuf, n // 2, (slot, 1))
        half = D_MODEL // 2
        hi_mask = jnp.uint32(0xFFFF0000)
        o_ref[:, :half] = h1_ref[:, :half] + (pltpu.bitcast(u0 << 16, F32) * g0 + pltpu.bitcast(u1 << 16, F32) * g1)
        o_ref[:, half:] = h1_ref[:, half:] + (pltpu.bitcast(u0 & hi_mask, F32) * g0
                                              + pltpu.bitcast(u1 & hi_mask, F32) * g1)

    for parity in range(2):
        pl.when((i < n_steps) & ((i & 1) == parity))(functools.partial(tile, parity))

    @pl.when(i == n_steps)
    def _():
        for a in range(n):
            row_wait(a, i & 1)


def _combine(dest3, h1, route, y):
    m = h1.shape[0]
    tm = dest3.shape[2] // 2
    n_steps = m // tm
    tok = lambda i: (jnp.minimum(i, n_steps - 1), 0)
    return pl.pallas_call(
        functools.partial(_combine_body, n_steps=n_steps),
        grid=(n_steps + 1,),
        in_specs=[
            pl.BlockSpec((1, 1, 2 * tm), lambda i: (i, 0, 0), memory_space=pltpu.SMEM),
            pl.BlockSpec((1, 1, 2 * tm), lambda i: (jnp.minimum(i + 1, n_steps), 0, 0), memory_space=pltpu.SMEM),
            pl.BlockSpec((tm, D_MODEL), tok),
            pl.BlockSpec((tm, LANES), tok),
            pl.BlockSpec(memory_space=pl.ANY),
        ],
        out_specs=pl.BlockSpec((tm, D_MODEL), tok),
        out_shape=jax.ShapeDtypeStruct((m, D_MODEL), F32),
        scratch_shapes=[pltpu.VMEM((2, 2, SUBLANES * tm, LANES), U32), pltpu.SemaphoreType.DMA((2,))],
        compiler_params=_cparams(("arbitrary",), 40),
        name="moe_combine",
    )(dest3, dest3, h1, route, y)


def _routing_tables(route, n_tok):
    a_expert = route[:, 2:4].astype(I32).reshape(-1)
    n_assign = a_expert.shape[0]
    eids = jnp.arange(N_EXPERTS, dtype=I32)
    sorted_e, order = lax.sort((a_expert, jnp.arange(n_assign, dtype=I32)), num_keys=1, is_stable=True)
    first = jnp.sum((sorted_e[None, :] < eids[:, None]).astype(I32), axis=1)
    counts = jnp.concatenate([first[1:], jnp.full((1,), n_assign, I32)]) - first
    padded = (counts + MOE_TILE - 1) // MOE_TILE * MOE_TILE
    upto = (eids[None, :] <= eids[:, None]).astype(I32)
    seg_end = jnp.sum(upto * padded[None, :], axis=1)
    seg_start = seg_end - padded
    shift = seg_start - first
    shift_p = jnp.sum(jnp.where(sorted_e[None, :] == eids[:, None], shift[:, None], 0), axis=0)
    _, dest = lax.sort((order, jnp.arange(n_assign, dtype=I32) + shift_p), num_keys=1)

    n_tiles = n_assign // MOE_TILE + N_EXPERTS
    n_used = (seg_end[-1] // MOE_TILE).astype(I32)
    tile_start = jnp.arange(n_tiles + 2, dtype=I32) * MOE_TILE
    te = jnp.sum((seg_end[None, :] <= tile_start[:, None]).astype(I32), axis=1)
    te = jnp.minimum(te, N_EXPERTS - 1).astype(I32)
    used = counts > 0
    later_used = used[None, :] & (eids[None, :] > eids[:, None])
    next_e = jnp.min(jnp.where(later_used, eids[None, :], N_EXPERTS), axis=1)
    next_e = jnp.where(next_e == N_EXPERTS, -1, next_e).astype(I32)
    ordinal = jnp.sum(upto * used.astype(I32)[None, :], axis=1) - 1
    base = first[te] + tile_start - seg_start[te]
    base = jnp.where(jnp.arange(n_tiles + 2) < n_used, jnp.clip(base, 0, n_assign), n_assign).astype(I32)
    tok_sorted = jnp.concatenate([order // 2, jnp.zeros((MOE_TILE,), I32)])
    return (dest.astype(I32), tok_sorted, base, te[:n_tiles], next_e[te[:n_tiles]],
            (ordinal[te[:n_tiles]] & 1).astype(I32), n_used.reshape(1))


def kernel(x, meta_tokens, rel_bias, ln_mix, w_in, q_norm, k_norm, sinks, out_norm_sb, out_norm_swa,
           w_out, ln_ffn, w_router_group, b_router_group, w_router_expert, b_router_expert,
           w_gate, w_up, w_down):
    batch, seq, _ = x.shape
    n_tok = batch * seq
    x2d = x.reshape(n_tok, D_MODEL)

    ones = lambda n: jnp.ones((n,), F32)
    col_scale = jnp.concatenate([
        ones(SB_WIDTH) * SCALE, ones(2 * SB_WIDTH),
        jnp.tile(q_norm[0], SWA_Q_HEADS) * SCALE, jnp.tile(k_norm[0], SWA_KV_HEADS), ones(SWA_KV_WIDTH)])[None]
    norm_flag = jnp.concatenate([jnp.zeros((3 * SB_WIDTH,), F32), ones(SWA_WIDTH + SWA_KV_WIDTH),
                                 jnp.zeros((SWA_KV_WIDTH,), F32)])[None]
    w_in_bf = w_in[0].astype(BF16)
    gain_mix = ln_mix[0][None]

    proj = _norm_proj(x2d, gain_mix, w_in_bf, col_scale, norm_flag)
    meta_proj = _norm_proj(meta_tokens.astype(F32), gain_mix, w_in_bf, col_scale, norm_flag)
    meta_blk = jnp.concatenate([jnp.zeros((META_PAD, IN_PROJ_WIDTH), BF16), meta_proj], axis=0)

    o_sb = _sb_attention(proj, meta_blk, batch, seq)
    o_sw = _swa_attention(proj, meta_blk, _bucket_tables(), rel_bias.astype(F32), sinks[0].astype(F32),
                          batch, seq)

    w_router = jnp.concatenate(
        [w_router_group[0], w_router_expert[0],
         jnp.zeros((D_MODEL, LANES - N_GROUPS - N_EXPERTS), F32)], axis=1).astype(BF16)
    b_router = jnp.concatenate(
        [b_router_group[0], b_router_expert[0], jnp.zeros((LANES - N_GROUPS - N_EXPERTS,), F32)])[None]
    h1, hfp, route = _out_router(o_sb, o_sw, x2d, out_norm_sb[0][None], out_norm_swa[0][None],
                                 w_out[0].astype(BF16), ln_ffn[0][None], w_router, b_router)

    dest, tok_sorted, tile_base, tile_expert, next_expert, w_slot, n_used = _routing_tables(route, n_tok)
    y = _experts(tile_expert, next_expert, w_slot, tile_base, n_used, tok_sorted, hfp,
                 w_gate[0], w_up[0], w_down[0])
    dest3 = jnp.concatenate([dest, jnp.zeros((2 * TOK_TILE,), I32)]).reshape(-1, 1, 2 * TOK_TILE)
    out = _combine(dest3, h1, route, y)
    return out.reshape(batch, seq, D_MODEL)
```
